```python
import jax, jax.numpy as jnp
from jax import lax
import numpy as np

D_MODEL = 1024
BATCH = 16
SEQ = 2048
DEPTH = 4
DEC_BATCH = 32
DEC_SEQ = 2048
PAST_LEN = 128

GRID_W = 64
N_MIXERS = 2
N_RWKV = (DEPTH + 1) // 2
N_ATTN = DEPTH // 2
RW_HEAD = 64
RW_HEADS = D_MODEL // RW_HEAD
LORA_W = 64
LORA_A = 64
LORA_V = 32
LORA_G = 128
GN_EPS = 64e-5
AT_HEAD = 128
AT_QH = D_MODEL // AT_HEAD
AT_KVH = 2
AT_GROUP = AT_QH // AT_KVH
Q_BLOCK = 128
ROPE_THETA = 10000.0
D_FF = 2816
NORM_EPS = 1e-6
F32 = jnp.float32

kernel_name = 'hybrid_rwkv7_axial_gqa_encoder'


def _rmsnorm(x, g, eps=NORM_EPS):
    xf = x.astype(F32)
    y = xf * lax.rsqrt(jnp.mean(xf * xf, axis=-1, keepdims=True) + eps)
    return (y * g.astype(F32)).astype(x.dtype)


def _shift_prev(z):
    return jnp.pad(z[:, :-1], ((0, 0), (1, 0), (0, 0)))


def _shift_next(z):
    return jnp.pad(z[:, 1:], ((0, 0), (0, 1), (0, 0)))


def _wkv_scan(r, w, k, v, a, b, reverse):
    Bn, T, H, N = r.shape
    xs = tuple(jnp.moveaxis(z, 1, 0) for z in (r, w, k, v, a, b))

    def step(S, inp):
        r_t, w_t, k_t, v_t, a_t, b_t = inp
        sa = jnp.einsum('bhij,bhj->bhi', S, a_t)
        S = S * w_t[:, :, None, :] + sa[..., None] * b_t[:, :, None, :] + v_t[..., None] * k_t[:, :, None, :]
        return S, jnp.einsum('bhij,bhj->bhi', S, r_t)

    S0 = jnp.zeros((Bn, H, N, N), F32)
    _, ys = lax.scan(step, S0, xs, reverse=reverse)
    return jnp.moveaxis(ys, 0, 1)


def _rwkv_mixer(h, v_first, mu, w_rkv, w0, wA, wB, a0, aA, aB, gA, gB, k_k, k_a, r_k, lnx, w_o, vres):
    Bn, T, D = h.shape
    H, N = RW_HEADS, RW_HEAD
    xx = 0.5 * (_shift_prev(h) + _shift_next(h)) - h
    xr, xw, xk, xv, xa, xg = (h + xx * mu[i] for i in range(6))
    rkv = jnp.einsum('nbtd,nde->nbte', jnp.stack([xr, xk, xv]), w_rkv)
    r, k, v = rkv[0], rkv[1], rkv[2]
    if vres is None:
        v_first = v
    else:
        v0, vA, vB = vres
        v = v + (v_first - v) * jax.nn.sigmoid(v0 + (xv @ vA) @ vB)
    w_log = w0[:, None, None, :] + jnp.einsum('nbtr,nrd->nbtd', jnp.tanh(jnp.einsum('btd,ndr->nbtr', xw, wA)), wB)
    w_log = -jax.nn.softplus(-w_log.astype(F32)) - 0.5
    decay = jnp.exp(-jnp.exp(w_log))
    a = jax.nn.sigmoid(a0[:, None, None, :] + jnp.einsum('nbtr,nrd->nbtd', jnp.einsum('btd,ndr->nbtr', xa, aA), aB))
    g = jax.nn.sigmoid(xg @ gA) @ gB
    kk = (k * k_k).reshape(Bn, T, H, N).astype(F32)
    kk = kk / jnp.maximum(jnp.sqrt(jnp.sum(kk * kk, axis=-1, keepdims=True)), 1e-12)
    kd = k[None] * (1 + (a - 1) * k_a)

    def hv(z):
        return z.reshape(z.shape[:-1] + (H, N)).astype(F32)

    r4, v4, a4, kd4, dec4 = hv(r), hv(v), hv(a), hv(kd), hv(decay)
    y = (_wkv_scan(r4, dec4[0], kd4[0], v4, -kk, kk * a4[0], reverse=False)
         + _wkv_scan(r4, dec4[1], kd4[1], v4, -kk, kk * a4[1], reverse=True))
    mean = jnp.mean(y, axis=-1, keepdims=True)
    var = jnp.mean(jnp.square(y - mean), axis=-1, keepdims=True)
    yn = (y - mean) * lax.rsqrt(var + GN_EPS)
    yn = yn * lnx[0].astype(F32).reshape(H, N) + lnx[1].astype(F32).reshape(H, N)
    bonus = jnp.sum(r4 * (kd4[0] + kd4[1]) * r_k.astype(F32), axis=-1, keepdims=True) * v4
    o = (yn + bonus).reshape(Bn, T, D).astype(h.dtype) * g
    return o @ w_o, v_first


def _axial_rope(T):
    rows = T // GRID_W
    row, col = jnp.meshgrid(jnp.arange(rows), jnp.arange(GRID_W), indexing='ij')
    row = row.reshape(-1).astype(F32)
    col = col.reshape(-1).astype(F32)
    n_pair = AT_HEAD // 4
    inv = ROPE_THETA ** (-jnp.arange(n_pair, dtype=F32) / n_pair)
    ang = jnp.concatenate([row[:, None] * inv, col[:, None] * inv], axis=-1)
    return jnp.cos(ang), jnp.sin(ang)


def _apply_rope(x, cos, sin):
    xf = x.astype(F32).reshape(x.shape[:-1] + (-1, 2))
    x0, x1 = xf[..., 0], xf[..., 1]
    c = cos[None, :, None, :]
    s = sin[None, :, None, :]
    out = jnp.stack([x0 * c - x1 * s, x0 * s + x1 * c], axis=-1).reshape(x.shape)
    return out.astype(x.dtype)


def _gqa_mixer(h, w_qkv, qk_g, w_o, cos, sin):
    Bn, T, D = h.shape
    nq, nk = AT_QH * AT_HEAD, AT_KVH * AT_HEAD
    qkv = h @ w_qkv
    q = qkv[..., :nq].reshape(Bn, T, AT_QH, AT_HEAD)
    k = qkv[..., nq:nq + nk].reshape(Bn, T, AT_KVH, AT_HEAD)
    v = qkv[..., nq + nk:].reshape(Bn, T, AT_KVH, AT_HEAD)
    q = _apply_rope(_rmsnorm(q, qk_g[0]), cos, sin)
    k = _apply_rope(_rmsnorm(k, qk_g[1]), cos, sin)
    scale = AT_HEAD ** -0.5
    qb = q.reshape(Bn, T // Q_BLOCK, Q_BLOCK, AT_KVH, AT_GROUP, AT_HEAD).transpose(1, 0, 2, 3, 4, 5)

    def block(qi):
        s = jnp.einsum('bqhgd,bkhd->bhgqk', qi, k).astype(F32) * scale
        p = jax.nn.softmax(s, axis=-1).astype(v.dtype)
        return jnp.einsum('bhgqk,bkhd->bqhgd', p, v)

    o = lax.map(block, qb)
    o = o.transpose(1, 0, 2, 3, 4, 5).reshape(Bn, T, D)
    return o @ w_o


def _conv_ffn(h, w_up, conv_w, conv_b, w_down):
    u, gv = jnp.split(h @ w_up, 2, axis=-1)
    u = conv_w[0] * _shift_prev(u) + conv_w[1] * u + conv_w[2] * _shift_next(u) + conv_b
    return (jax.nn.silu(u) * gv) @ w_down


def _trunk(x, c, P):
    cos, sin = _axial_rope(x.shape[1])
    sc = jax.nn.silu(c)
    v_first = None
    for l in range(DEPTH):
        mod = sc @ P['ada_w'][l] + P['ada_b'][l]
        sh_m, sc_m, g_m, sh_f, sc_f, g_f = (m[:, None, :] for m in jnp.split(mod, 6, axis=-1))
        ng = P['norm_g'][l]
        h = _rmsnorm(x, ng[0]) * (1 + sc_m) + sh_m
        j = l // N_MIXERS
        if l % N_MIXERS == 0:
            vres = None if j == 0 else (P['rw_v0'][j - 1], P['rw_vA'][j - 1], P['rw_vB'][j - 1])
            out, v_first = _rwkv_mixer(h, v_first, P['rw_mu'][j], P['rw_rkv'][j], P['rw_w0'][j], P['rw_wA'][j],
                                       P['rw_wB'][j], P['rw_a0'][j], P['rw_aA'][j], P['rw_aB'][j], P['rw_gA'][j],
                                       P['rw_gB'][j], P['rw_kk'][j], P['rw_ka'][j], P['rw_rk'][j], P['rw_lnx'][j],
                                       P['rw_o'][j], vres)
        else:
            out = _gqa_mixer(h, P['at_qkv'][j], P['at_qk_g'][j], P['at_o'][j], cos, sin)
        x = x + g_m * _rmsnorm(out, ng[1])
        h = _rmsnorm(x, ng[2]) * (1 + sc_f) + sh_f
        f = _conv_ffn(h, P['ffn_up'][l], P['ffn_conv_w'][l], P['ffn_conv_b'][l], P['ffn_down'][l])
        x = x + g_f * _rmsnorm(f, ng[3])
    return x


def setup_inputs(seed: int = 0) -> dict:
    key = jax.random.key(seed)
    ks = iter(jax.random.split(key, 48))
    D, F, H, N = D_MODEL, D_FF, RW_HEADS, RW_HEAD
    NR, NA = N_RWKV, N_ATTN

    def nrm(shape, scale):
        return jax.random.normal(next(ks), shape, F32) * scale

    def uni(shape, lo, hi):
        return jax.random.uniform(next(ks), shape, F32, lo, hi)

    qkv_w = (AT_QH + 2 * AT_KVH) * AT_HEAD
    return {
        'x_prompt': nrm((BATCH, SEQ, D), 1.0),
        'x_sample': nrm((DEC_BATCH, DEC_SEQ, D), 1.0),
        'c_prompt': nrm((BATCH, D), 1.0),
        'c_sample': nrm((DEC_BATCH, D), 1.0),
        'ada_w': nrm((DEPTH, D, 6 * D), 0.5 * D ** -0.5),
        'ada_b': nrm((DEPTH, 6 * D), 0.02),
        'norm_g': 1.0 + nrm((DEPTH, 4, D), 0.02),
        'rw_mu': uni((NR, 6, D), 0.0, 1.0),
        'rw_rkv': nrm((NR, 3, D, D), D ** -0.5),
        'rw_w0': uni((NR, 2, D), -6.0, -1.0),
        'rw_wA': nrm((NR, 2, D, LORA_W), D ** -0.5),
        'rw_wB': nrm((NR, 2, LORA_W, D), 0.1 * LORA_W ** -0.5),
        'rw_a0': nrm((NR, 2, D), 0.1),
        'rw_aA': nrm((NR, 2, D, LORA_A), D ** -0.5),
        'rw_aB': nrm((NR, 2, LORA_A, D), 0.1 * LORA_A ** -0.5),
        'rw_gA': nrm((NR, D, LORA_G), D ** -0.5),
        'rw_gB': nrm((NR, LORA_G, D), LORA_G ** -0.5),
        'rw_kk': 0.85 + nrm((NR, D), 0.02),
        'rw_ka': 1.0 + nrm((NR, D), 0.02),
        'rw_rk': nrm((NR, H, N), 0.1),
        'rw_lnx': jnp.stack([1.0 + nrm((NR, D), 0.02), nrm((NR, D), 0.02)], axis=1),
        'rw_o': nrm((NR, D, D), D ** -0.5),
        'rw_v0': nrm((NR - 1, D), 0.1),
        'rw_vA': nrm((NR - 1, D, LORA_V), D ** -0.5),
        'rw_vB': nrm((NR - 1, LORA_V, D), LORA_V ** -0.5),
        'at_qkv': nrm((NA, D, qkv_w), D ** -0.5),
        'at_qk_g': 1.0 + nrm((NA, 2, AT_HEAD), 0.02),
        'at_o': nrm((NA, D, D), D ** -0.5),
        'ffn_up': nrm((DEPTH, D, 2 * F), D ** -0.5),
        'ffn_conv_w': nrm((DEPTH, 3, F), 3 ** -0.5),
        'ffn_conv_b': nrm((DEPTH, F), 0.02),
        'ffn_down': nrm((DEPTH, F, D), F ** -0.5),
    }


def reference(x_prompt, x_sample, c_prompt, c_sample, ada_w, ada_b, norm_g, rw_mu, rw_rkv, rw_w0, rw_wA, rw_wB,
              rw_a0, rw_aA, rw_aB, rw_gA, rw_gB, rw_kk, rw_ka, rw_rk, rw_lnx, rw_o, rw_v0, rw_vA, rw_vB,
              at_qkv, at_qk_g, at_o, ffn_up, ffn_conv_w, ffn_conv_b, ffn_down):
    P = dict(ada_w=ada_w, ada_b=ada_b, norm_g=norm_g, rw_mu=rw_mu, rw_rkv=rw_rkv, rw_w0=rw_w0, rw_wA=rw_wA,
             rw_wB=rw_wB, rw_a0=rw_a0, rw_aA=rw_aA, rw_aB=rw_aB, rw_gA=rw_gA, rw_gB=rw_gB, rw_kk=rw_kk,
             rw_ka=rw_ka, rw_rk=rw_rk, rw_lnx=rw_lnx, rw_o=rw_o, rw_v0=rw_v0, rw_vA=rw_vA, rw_vB=rw_vB,
             at_qkv=at_qkv, at_qk_g=at_qk_g, at_o=at_o, ffn_up=ffn_up, ffn_conv_w=ffn_conv_w,
             ffn_conv_b=ffn_conv_b, ffn_down=ffn_down)
    y_prompt = _trunk(x_prompt, c_prompt, P)
    y_sample = _trunk(x_sample, c_sample, P)
    return (y_prompt, y_sample)
```

```python
import functools

import jax
import jax.numpy as jnp
from jax import lax
from jax.experimental import pallas as pl
from jax.experimental.pallas import tpu as pltpu

F32 = jnp.float32
BF16 = jnp.bfloat16

RW_HEAD = 64
AT_HEAD = 128
AT_KVH = 2
GRID_W = 64
ROPE_THETA = 10000.0
NORM_EPS = 1e-6
GN_EPS = 64e-5
N_MOD = 6

LANES = 128
HEAD_SHIFT = RW_HEAD.bit_length() - 1
WKV_CHUNK = 64
WKV_BLOCK = 128
ROW_TILE = 256
ATT_Q_TILE = 256
FFN_TILE = 256
VMEM_LIMIT = 56 * 1024 * 1024


def _dot(a, b):
    return jnp.dot(a, b, preferred_element_type=F32)


def _dot_nt(a, b):
    return lax.dot_general(a, b, (((1,), (1,)), ((), ())), preferred_element_type=F32)


def _dot_tn(a, b):
    return lax.dot_general(a, b, (((0,), (0,)), ((), ())), preferred_element_type=F32)


def _bf(x):
    return x.astype(BF16)


def _sigmoid(x):
    return 1.0 / (1.0 + jnp.exp(-x))


def _norm_mod(x, gain, shift, scale):
    ms = jnp.mean(x * x, axis=-1, keepdims=True)
    return x * lax.rsqrt(ms + NORM_EPS) * gain * (1.0 + scale) + shift


def _rms(x, gain):
    ms = jnp.mean(x * x, axis=-1, keepdims=True)
    return x * lax.rsqrt(ms + NORM_EPS) * gain


def _head_ones():
    r = lax.broadcasted_iota(jnp.int32, (LANES, LANES), 0) >> HEAD_SHIFT
    c = lax.broadcasted_iota(jnp.int32, (LANES, LANES), 1) >> HEAD_SHIFT
    return jnp.where(r == c, 1.0, 0.0).astype(BF16)


def _head_sum(x, ones_bd):
    hi = _bf(x)
    lo = _bf(x - hi.astype(F32))
    out = []
    for p in range(x.shape[1] // LANES):
        sl = slice(p * LANES, (p + 1) * LANES)
        out.append(_dot(hi[:, sl], ones_bd) + _dot(lo[:, sl], ones_bd))
    return jnp.concatenate(out, axis=1)


def _shift_rows_down(x, first_row):
    y = pltpu.roll(x, 1, axis=0)
    row = lax.broadcasted_iota(jnp.int32, (8, x.shape[1]), 0)
    head = jnp.where(row == 0, first_row, y[:8])
    return jnp.concatenate([head, y[8:]], axis=0)


def _shift_rows_up(x, last_row):
    n = x.shape[0]
    y = pltpu.roll(x, n - 1, axis=0)
    row = lax.broadcasted_iota(jnp.int32, (8, x.shape[1]), 0)
    tail = jnp.where(row == 7, last_row, y[n - 8:])
    return jnp.concatenate([y[:n - 8], tail], axis=0)


def _cparams(sem):
    return pltpu.CompilerParams(dimension_semantics=sem, vmem_limit_bytes=VMEM_LIMIT)


def _mod_kernel(c_ref, w_ref, b_ref, o_ref):
    c = c_ref[...]
    sc = c * _sigmoid(c)
    o_ref[0] = _dot(_bf(sc), _bf(w_ref[0])) + b_ref[0]


def _ada_mod(c, ada_w, ada_b):
    depth, d, n = ada_w.shape
    b = c.shape[0]
    tn = n // 4
    return pl.pallas_call(
        _mod_kernel,
        grid=(depth, n // tn),
        in_specs=[
            pl.BlockSpec((b, d), lambda l, j: (0, 0)),
            pl.BlockSpec((1, d, tn), lambda l, j: (l, 0, j)),
            pl.BlockSpec((1, 1, tn), lambda l, j: (l, 0, j)),
        ],
        out_specs=pl.BlockSpec((1, b, tn), lambda l, j: (l, 0, j)),
        out_shape=jax.ShapeDtypeStruct((depth, b, n), F32),
        compiler_params=_cparams(("arbitrary", "arbitrary")),
        name="ada_mod",
    )(c, ada_w, ada_b.reshape(depth, 1, n))


def _softplus(z):
    return jnp.maximum(z, 0.0) + jnp.log(1.0 + jnp.exp(-jnp.abs(z)))


def _rw_proj_kernel(has_vres, x_ref, xp_ref, xn_ref, mod_ref, ng_ref, mu_ref, wrkv_ref, w0_ref, wA_ref, wB_ref,
                    a0_ref, aA_ref, aB_ref, gA_ref, gB_ref, kk_ref, ka_ref, rk_ref, *rest):
    if has_vres:
        vf_ref, v0_ref, vA_ref, vB_ref = rest[:4]
        rest = rest[4:]
    r_o, k_o, v_o, kkn_o, g_o, bonus_o, lw0_o, lw1_o, as0_o, as1_o = rest
    i = pl.program_id(1)
    nt = pl.num_programs(1)
    mod = mod_ref[0]
    shift, scale = mod[0:1], mod[1:2]
    gain = ng_ref[...]
    h = _norm_mod(x_ref[0], gain, shift, scale)
    h_before = _norm_mod(xp_ref[0], gain, shift, scale)[7:8]
    h_after = _norm_mod(xn_ref[0], gain, shift, scale)[0:1]
    h_before = jnp.where(i == 0, 0.0, h_before)
    h_after = jnp.where(i == nt - 1, 0.0, h_after)
    xx = 0.5 * (_shift_rows_down(h, h_before) + _shift_rows_up(h, h_after)) - h
    mu = mu_ref[...]
    xr, xw, xk, xv, xa, xg = (_bf(h + xx * mu[n:n + 1]) for n in range(6))

    r = _dot(xr, wrkv_ref[0])
    k = _dot(xk, wrkv_ref[1])
    v = _dot(xv, wrkv_ref[2])
    if has_vres:
        mix = _sigmoid(v0_ref[...] + _dot(_bf(_dot(xv, vA_ref[...])), vB_ref[...]))
        v = v + (vf_ref[0] - v) * mix
    r_o[0] = r
    k_o[0] = k
    v_o[0] = v

    g_o[0] = _dot(_bf(_sigmoid(_dot(xg, gA_ref[...]))), gB_ref[...])

    ones_bd = _head_ones()
    kk = k * kk_ref[...]
    kk_norm = jnp.sqrt(_head_sum(kk * kk, ones_bd))
    kkn_o[0] = kk / jnp.maximum(kk_norm, 1e-12)

    ka = ka_ref[...]
    kd_sum = jnp.zeros_like(k)
    for n, (lw_o, as_o) in enumerate(((lw0_o, as0_o), (lw1_o, as1_o))):
        wl = w0_ref[n:n + 1] + _dot(_bf(jnp.tanh(_dot(xw, wA_ref[n]))), wB_ref[n])
        w_log = -_softplus(-wl) - 0.5
        lw_o[0] = -jnp.exp(w_log)
        a_sig = _sigmoid(a0_ref[n:n + 1] + _dot(_bf(_dot(xa, aA_ref[n])), aB_ref[n]))
        as_o[0] = a_sig
        kd_sum = kd_sum + k * (1.0 + (a_sig - 1.0) * ka)
    bonus_o[0] = _head_sum(r * kd_sum * rk_ref[...], ones_bd) * v


def _rw_proj(x, mod, ng, mu, w_rkv, w0, wA, wB, a0, aA, aB, gA, gB, k_k, k_a, r_k, vres):
    b, t, d = x.shape
    tq = min(ROW_TILE, t)
    nt = t // tq
    has_vres = vres is not None
    row = pl.BlockSpec((1, tq, d), lambda bi, i: (bi, i, 0))

    def full(a):
        nd = a.ndim
        return pl.BlockSpec(a.shape, lambda bi, i: (0,) * nd)

    halo = tq // 8
    n8 = t // 8
    in_specs = [
        row,
        pl.BlockSpec((1, 8, d), lambda bi, i: (bi, jnp.maximum(i * halo - 1, 0), 0)),
        pl.BlockSpec((1, 8, d), lambda bi, i: (bi, jnp.minimum((i + 1) * halo, n8 - 1), 0)),
        pl.BlockSpec((1, N_MOD, d), lambda bi, i: (bi, 0, 0)),
    ]
    params = [ng, mu, w_rkv, w0, wA, wB, a0, aA, aB, gA, gB, k_k, k_a, r_k]
    args = [x, x, x, mod] + params
    in_specs += [full(a) for a in params]
    if has_vres:
        v_first, v0, vA, vB = vres
        args += [v_first, v0, vA, vB]
        in_specs += [row, full(v0), full(vA), full(vB)]
    n_out = 10
    return pl.pallas_call(
        functools.partial(_rw_proj_kernel, has_vres),
        grid=(b, nt),
        in_specs=in_specs,
        out_specs=[row] * n_out,
        out_shape=[jax.ShapeDtypeStruct((b, t, d), F32)] * n_out,
        compiler_params=_cparams(("arbitrary", "arbitrary")),
        name="rw_proj",
    )(*args)


def _wkv_kernel(reverse, r_ref, k_ref, v_ref, kk_ref, lw_ref, as_ref, ka_ref, tri_ref, y_ref, s_ref):
    L = WKV_CHUNK
    rows, d = lw_ref.shape[1], lw_ref.shape[2]
    n_chunk = rows // L
    n_tile = d // LANES

    @pl.when(pl.program_id(1) == 0)
    def _():
        s_ref[...] = jnp.zeros_like(s_ref)

    lw = lw_ref[0]
    hi = _bf(lw)
    rem = lw - hi.astype(F32)
    mid = _bf(rem)
    lo = _bf(rem - mid.astype(F32))
    tri = tri_ref[...]
    cum = _dot(tri, hi) + _dot(tri, mid) + _dot(tri, lo)
    cs, cs_end = cum[:rows], cum[rows:]

    w_in = jnp.exp(cs)
    w_inv = jnp.exp(-cs)
    w_ex = w_in * jnp.exp(-lw)
    dec = jnp.exp(cs_end)
    kk = kk_ref[0]
    a_sig = as_ref[0]
    v = v_ref[0]
    a_t = -(kk * w_ex)
    r_t = r_ref[0] * w_in
    b_t = kk * a_sig * w_inv
    k_t = k_ref[0] * (1.0 + (a_sig - 1.0) * ka_ref[...]) * w_inv
    b_h = b_t * dec
    k_h = k_t * dec

    lane = lax.broadcasted_iota(jnp.int32, (L, LANES), 1)
    row = lax.broadcasted_iota(jnp.int32, (L, LANES), 0)
    col = lane & (L - 1)
    head0 = lane < RW_HEAD
    if reverse:
        strict, incl = row < col, row <= col
    else:
        strict, incl = row > col, row >= col
    eye2 = jnp.where(row == col, 1.0, 0.0)
    r2 = lax.broadcasted_iota(jnp.int32, (LANES, LANES), 0) >> HEAD_SHIFT
    c2 = lax.broadcasted_iota(jnp.int32, (LANES, LANES), 1) >> HEAD_SHIFT
    same_head = r2 == c2

    def bdiag(z):
        return _bf(jnp.concatenate([jnp.where(head0, z, 0.0), jnp.where(head0, 0.0, z)], axis=0))

    chunks = range(n_chunk - 1, -1, -1) if reverse else range(n_chunk)
    for c in chunks:
        rs = slice(c * L, (c + 1) * L)
        for p in range(n_tile):
            ls = slice(p * LANES, (p + 1) * LANES)
            at, rt, bt, kt = a_t[rs, ls], r_t[rs, ls], b_t[rs, ls], k_t[rs, ls]
            vv, bh, kh = v[rs, ls], b_h[rs, ls], k_h[rs, ls]
            rhs = jnp.concatenate([bdiag(bt), bdiag(kt)], axis=0)
            gram = _dot_nt(_bf(jnp.concatenate([at, rt], axis=0)), rhs)
            a_ab = jnp.where(strict, gram[:L, :LANES], 0.0)
            a_ak = jnp.where(strict, gram[:L, LANES:], 0.0)
            a_rb = jnp.where(incl, gram[L:, :LANES], 0.0)
            a_rk = jnp.where(incl, gram[L:, LANES:], 0.0)
            x_inv = eye2 + jnp.where((row >> 1) == (col >> 1), a_ab, 0.0)
            lev = 1
            while (1 << lev) < L:
                sel = ((row >> (lev + 1)) == (col >> (lev + 1))) & ((row >> lev) != (col >> lev))
                step = _dot(_bf(x_inv), bdiag(jnp.where(sel, a_ab, 0.0)))
                x_inv = x_inv + _dot(_bf(step), bdiag(x_inv))
                lev += 1
            xb = _bf(x_inv)
            akv = _dot(_bf(a_ak), bdiag(vv))
            sol = _dot(xb, jnp.concatenate([bdiag(at), bdiag(akv)], axis=1))
            at2, u0 = sol[:, :LANES], sol[:, LANES:]
            mix = _dot(_bf(a_rb), jnp.concatenate([bdiag(at2), bdiag(u0)], axis=1))
            r2_ = rt + mix[:, :LANES]
            y0 = mix[:, LANES:] + _dot(_bf(a_rk), bdiag(vv))
            p_bd = jnp.where(same_head, _dot_tn(_bf(at2), _bf(bh)), 0.0)
            q_bd = jnp.where(
                same_head,
                _dot_tn(_bf(jnp.concatenate([u0, vv], axis=0)), _bf(jnp.concatenate([bh, kh], axis=0))),
                0.0)
            s0 = s_ref[p]
            s0b = _bf(s0)
            y_ref[0, rs, ls] = _dot_nt(_bf(r2_), s0b) + y0
            s_ref[p] = s0 * dec[c * L:c * L + 1, ls] + _dot(s0b, _bf(p_bd)) + q_bd


def _wkv_tri(rows, reverse):
    t = jnp.arange(rows)
    same = (t[:, None] // WKV_CHUNK) == (t[None, :] // WKV_CHUNK)
    tri = (t[:, None] <= t[None, :]) if reverse else (t[:, None] >= t[None, :])
    return jnp.concatenate([same & tri, same], axis=0).astype(BF16)


def _wkv(r, k, v, kk, lw, a_sig, k_a, reverse):
    b, t, d = r.shape
    rows = min(WKV_BLOCK, t)
    nb = t // rows
    if reverse:
        blk = pl.BlockSpec((1, rows, d), lambda bi, i: (bi, nb - 1 - i, 0))
    else:
        blk = pl.BlockSpec((1, rows, d), lambda bi, i: (bi, i, 0))
    return pl.pallas_call(
        functools.partial(_wkv_kernel, reverse),
        grid=(b, nb),
        in_specs=[blk] * 6 + [
            pl.BlockSpec((1, d), lambda bi, i: (0, 0)),
            pl.BlockSpec((2 * rows, rows), lambda bi, i: (0, 0)),
        ],
        out_specs=blk,
        out_shape=jax.ShapeDtypeStruct((b, t, d), F32),
        scratch_shapes=[pltpu.VMEM((d // LANES, LANES, LANES), F32)],
        compiler_params=_cparams(("arbitrary", "arbitrary")),
        name="wkv_bwd" if reverse else "wkv_fwd",
    )(r, k, v, kk, lw, a_sig, k_a, _wkv_tri(rows, reverse))


def _rw_out_kernel(yf_ref, yb_ref, bonus_ref, g_ref, x_ref, mod_ref, ng_ref, lnx_ref, wo_ref, o_ref):
    ones_bd = _head_ones()
    y = yf_ref[0] + yb_ref[0]
    inv_n = 1.0 / RW_HEAD
    mean = _head_sum(y, ones_bd) * inv_n
    yc = y - mean
    var = _head_sum(yc * yc, ones_bd) * inv_n
    lnx = lnx_ref[...]
    yn = yc * lax.rsqrt(var + GN_EPS) * lnx[0:1] + lnx[1:2]
    o = (yn + bonus_ref[0]) * g_ref[0]
    out = _dot(_bf(o), wo_ref[...])
    gate = mod_ref[0][2:3]
    o_ref[0] = x_ref[0] + gate * _rms(out, ng_ref[...])


def _rw_out(y_f, y_b, bonus, g, x, mod, ng, lnx, w_o):
    b, t, d = x.shape
    tq = min(ROW_TILE, t)
    row = pl.BlockSpec((1, tq, d), lambda bi, i: (bi, i, 0))
    return pl.pallas_call(
        _rw_out_kernel,
        grid=(b, t // tq),
        in_specs=[row] * 5 + [
            pl.BlockSpec((1, N_MOD, d), lambda bi, i: (bi, 0, 0)),
            pl.BlockSpec((1, d), lambda bi, i: (0, 0)),
            pl.BlockSpec((2, d), lambda bi, i: (0, 0)),
            pl.BlockSpec((d, d), lambda bi, i: (0, 0)),
        ],
        out_specs=row,
        out_shape=jax.ShapeDtypeStruct((b, t, d), F32),
        compiler_params=_cparams(("arbitrary", "arbitrary")),
        name="rw_out",
    )(y_f, y_b, bonus, g, x, mod, ng, lnx, w_o)


def _rope(x, cos, sin_signed):
    lane = lax.broadcasted_iota(jnp.int32, x.shape, 1)
    partner = jnp.where((lane & 1) == 0, pltpu.roll(x, LANES - 1, axis=1), pltpu.roll(x, 1, axis=1))
    return x * cos + partner * sin_signed


def _at_qkv_kernel(x_ref, mod_ref, ng_ref, w_ref, qkg_ref, cos_ref, sin_ref, q_o, k_o, v_o):
    mod = mod_ref[0]
    h = _norm_mod(x_ref[0], ng_ref[...], mod[0:1], mod[1:2])
    qkv = _dot(_bf(h), w_ref[...])
    nq = q_o.shape[2]
    nk = k_o.shape[2]
    cos, sin = cos_ref[...], sin_ref[...]
    qkg = qkg_ref[...]
    scale = AT_HEAD ** -0.5

    def head(z, gain, mul):
        ms = jnp.mean(z * z, axis=-1, keepdims=True)
        return _rope(z * lax.rsqrt(ms + NORM_EPS) * gain, cos, sin) * mul

    for n in range(nq // AT_HEAD):
        sl = slice(n * AT_HEAD, (n + 1) * AT_HEAD)
        q_o[0, :, sl] = _bf(head(qkv[:, sl], qkg[0:1], scale))
    for n in range(nk // AT_HEAD):
        sl = slice(n * AT_HEAD, (n + 1) * AT_HEAD)
        k_o[0, :, sl] = _bf(head(qkv[:, nq + n * AT_HEAD:nq + (n + 1) * AT_HEAD], qkg[1:2], 1.0))
    v_o[0] = _bf(qkv[:, nq + nk:])


def _at_qkv(x, mod, ng, w_qkv, qk_g, cos, sin):
    b, t, d = x.shape
    tq = min(ROW_TILE, t)
    nk = AT_KVH * AT_HEAD
    row = lambda n: pl.BlockSpec((1, tq, n), lambda bi, i: (bi, i, 0))
    return pl.pallas_call(
        _at_qkv_kernel,
        grid=(b, t // tq),
        in_specs=[
            row(d),
            pl.BlockSpec((1, N_MOD, d), lambda bi, i: (bi, 0, 0)),
            pl.BlockSpec((1, d), lambda bi, i: (0, 0)),
            pl.BlockSpec(w_qkv.shape, lambda bi, i: (0, 0)),
            pl.BlockSpec((2, AT_HEAD), lambda bi, i: (0, 0)),
            pl.BlockSpec((tq, AT_HEAD), lambda bi, i: (i, 0)),
            pl.BlockSpec((tq, AT_HEAD), lambda bi, i: (i, 0)),
        ],
        out_specs=[row(d), row(nk), row(nk)],
        out_shape=[jax.ShapeDtypeStruct((b, t, d), BF16), jax.ShapeDtypeStruct((b, t, nk), BF16),
                   jax.ShapeDtypeStruct((b, t, nk), BF16)],
        compiler_params=_cparams(("arbitrary", "arbitrary")),
        name="at_qkv",
    )(x, mod, ng, w_qkv, qk_g, cos, sin)


def _attn_kernel(q_ref, k_ref, v_ref, o_ref):
    tq = q_ref.shape[1]
    group = q_ref.shape[2] // AT_HEAD
    q = jnp.concatenate([q_ref[0, :, g * AT_HEAD:(g + 1) * AT_HEAD] for g in range(group)], axis=0)
    s = _dot_nt(q, k_ref[0])
    p = jnp.exp(s - jnp.max(s, axis=-1, keepdims=True))
    denom = jnp.sum(p, axis=-1, keepdims=True)
    o = _dot(_bf(p), v_ref[0]) / denom
    for g in range(group):
        o_ref[0, :, g * AT_HEAD:(g + 1) * AT_HEAD] = _bf(o[g * tq:(g + 1) * tq])


def _attention(q, k, v):
    b, t, d = q.shape
    group = d // AT_HEAD // AT_KVH
    gw = group * AT_HEAD
    tq = min(ATT_Q_TILE, t)
    qblk = pl.BlockSpec((1, tq, gw), lambda bi, h, i: (bi, i, h))
    kvblk = pl.BlockSpec((1, t, AT_HEAD), lambda bi, h, i: (bi, 0, h))
    return pl.pallas_call(
        _attn_kernel,
        grid=(b, AT_KVH, t // tq),
        in_specs=[qblk, kvblk, kvblk],
        out_specs=qblk,
        out_shape=jax.ShapeDtypeStruct((b, t, d), BF16),
        compiler_params=_cparams(("arbitrary", "arbitrary", "arbitrary")),
        name="attention",
    )(q, k, v)


def _at_out_kernel(a_ref, x_ref, mod_ref, ng_ref, wo_ref, o_ref):
    out = _dot(a_ref[0], wo_ref[...])
    gate = mod_ref[0][2:3]
    o_ref[0] = x_ref[0] + gate * _rms(out, ng_ref[...])


def _at_out(a, x, mod, ng, w_o):
    b, t, d = x.shape
    tq = min(ROW_TILE, t)
    row = pl.BlockSpec((1, tq, d), lambda bi, i: (bi, i, 0))
    return pl.pallas_call(
        _at_out_kernel,
        grid=(b, t // tq),
        in_specs=[row, row,
                  pl.BlockSpec((1, N_MOD, d), lambda bi, i: (bi, 0, 0)),
                  pl.BlockSpec((1, d), lambda bi, i: (0, 0)),
                  pl.BlockSpec((d, d), lambda bi, i: (0, 0))],
        out_specs=row,
        out_shape=jax.ShapeDtypeStruct((b, t, d), F32),
        compiler_params=_cparams(("arbitrary", "arbitrary")),
        name="at_out",
    )(a, x, mod, ng, w_o)


def _ffn_kernel(x_ref, mod_ref, ng_ref, wu_ref, wg_ref, cw_ref, cb_ref, wd_ref, o_ref, h_ref):
    f = pl.program_id(1)
    mod = mod_ref[0]

    @pl.when(f == 0)
    def _():
        ng = ng_ref[...]
        h_ref[...] = _bf(_norm_mod(x_ref[0], ng[0:1], mod[3:4], mod[4:5]))
        o_ref[...] = jnp.zeros_like(o_ref)

    h = h_ref[...]
    u = _dot(h, wu_ref[0])
    gv = _dot(h, wg_ref[0])
    cw = cw_ref[...]
    zero = jnp.zeros((1, u.shape[1]), F32)
    u = cw[0:1] * _shift_rows_down(u, zero) + cw[1:2] * u + cw[2:3] * _shift_rows_up(u, zero) + cb_ref[...]
    act = u * _sigmoid(u) * gv
    o_ref[0] += _dot(_bf(act), wd_ref[...])

    @pl.when(f == pl.num_programs(1) - 1)
    def _():
        ng = ng_ref[...]
        o_ref[0] = x_ref[0] + mod[5:6] * _rms(o_ref[0], ng[1:2])


def _ffn(x, mod, ng, w_up, conv_w, conv_b, w_down):
    b, t, d = x.shape
    f_dim = w_down.shape[0]
    ft = min(FFN_TILE, f_dim)
    nf = f_dim // ft
    seq = pl.BlockSpec((1, t, d), lambda bi, f: (bi, 0, 0))
    return pl.pallas_call(
        _ffn_kernel,
        grid=(b, nf),
        in_specs=[
            seq,
            pl.BlockSpec((1, N_MOD, d), lambda bi, f: (bi, 0, 0)),
            pl.BlockSpec((2, d), lambda bi, f: (0, 0)),
            pl.BlockSpec((1, d, ft), lambda bi, f: (0, 0, f)),
            pl.BlockSpec((1, d, ft), lambda bi, f: (1, 0, f)),
            pl.BlockSpec((3, ft), lambda bi, f: (0, f)),
            pl.BlockSpec((1, ft), lambda bi, f: (0, f)),
            pl.BlockSpec((ft, d), lambda bi, f: (f, 0)),
        ],
        out_specs=seq,
        out_shape=jax.ShapeDtypeStruct((b, t, d), F32),
        scratch_shapes=[pltpu.VMEM((t, d), BF16)],
        compiler_params=_cparams(("arbitrary", "arbitrary")),
        name="ffn",
    )(x, mod, ng, w_up, w_up, conv_w, conv_b.reshape(1, f_dim), w_down)


def _rope_tables(t):
    pos = jnp.arange(t)
    row = (pos // GRID_W).astype(F32)
    col = (pos % GRID_W).astype(F32)
    n_pair = AT_HEAD // 4
    inv = ROPE_THETA ** (-jnp.arange(n_pair, dtype=F32) / n_pair)
    ang = jnp.concatenate([row[:, None] * inv, col[:, None] * inv], axis=-1)
    cos = jnp.repeat(jnp.cos(ang), 2, axis=-1)
    sin = jnp.repeat(jnp.sin(ang), 2, axis=-1)
    sign = jnp.where(jnp.arange(AT_HEAD) % 2 == 0, -1.0, 1.0)
    return cos, sin * sign


def kernel(x_prompt, x_sample, c_prompt, c_sample, ada_w, ada_b, norm_g, rw_mu, rw_rkv, rw_w0, rw_wA, rw_wB, rw_a0,
           rw_aA, rw_aB, rw_gA, rw_gB, rw_kk, rw_ka, rw_rk, rw_lnx, rw_o, rw_v0, rw_vA, rw_vB, at_qkv, at_qk_g,
           at_o, ffn_up, ffn_conv_w, ffn_conv_b, ffn_down):
    assert x_prompt.shape[1:] == x_sample.shape[1:]
    n_prompt = x_prompt.shape[0]
    x = jnp.concatenate([x_prompt, x_sample], axis=0)
    c = jnp.concatenate([c_prompt, c_sample], axis=0)
    b, t, d = x.shape
    depth = ada_w.shape[0]
    f_dim = ffn_down.shape[1]

    mod_all = _ada_mod(c, ada_w, ada_b).reshape(depth, b, N_MOD, d)
    cos, sin = _rope_tables(t)
    v_first = None
    for l in range(depth):
        mod = mod_all[l]
        ng = norm_g[l]
        j = l // 2
        if l % 2 == 0:
            vres = None
            if j > 0:
                vres = (v_first, rw_v0[j - 1].reshape(1, d), _bf(rw_vA[j - 1]), _bf(rw_vB[j - 1]))
            r, k, v, kk, g, bonus, lw0, lw1, as0, as1 = _rw_proj(
                x, mod, ng[0:1], rw_mu[j], _bf(rw_rkv[j]), rw_w0[j], _bf(rw_wA[j]), _bf(rw_wB[j]), rw_a0[j],
                _bf(rw_aA[j]), _bf(rw_aB[j]), _bf(rw_gA[j]), _bf(rw_gB[j]), rw_kk[j].reshape(1, d),
                rw_ka[j].reshape(1, d), rw_rk[j].reshape(1, d), vres)
            if j == 0:
                v_first = v
            ka = rw_ka[j].reshape(1, d)
            y_f = _wkv(r, k, v, kk, lw0, as0, ka, reverse=False)
            y_b = _wkv(r, k, v, kk, lw1, as1, ka, reverse=True)
            x = _rw_out(y_f, y_b, bonus, g, x, mod, ng[1:2], rw_lnx[j], _bf(rw_o[j]))
        else:
            q, k, v = _at_qkv(x, mod, ng[0:1], _bf(at_qkv[j]), at_qk_g[j], cos, sin)
            a = _attention(q, k, v)
            x = _at_out(a, x, mod, ng[1:2], _bf(at_o[j]))
        w_up = _bf(ffn_up[l]).reshape(d, 2, f_dim).transpose(1, 0, 2)
        x = _ffn(x, mod, ng[2:4], w_up, ffn_conv_w[l], ffn_conv_b[l], _bf(ffn_down[l]))
    return (x[:n_prompt], x[n_prompt:])
```

```python
import functools

import jax
import jax.numpy as jnp
from jax import lax
from jax.experimental import pallas as pl
from jax.experimental.pallas import tpu as pltpu

F32 = jnp.float32
BF16 = jnp.bfloat16

RW_HEAD = 64
AT_HEAD = 128
AT_KVH = 2
GRID_W = 64
ROPE_THETA = 10000.0
NORM_EPS = 1e-6
GN_EPS = 64e-5
N_MOD = 6

LANES = 128
HEAD_SHIFT = RW_HEAD.bit_length() - 1
WKV_CHUNK = 64
WKV_BLOCK = 256
ROW_TILE = 256
ATT_Q_TILE = 256
FFN_TILE = 256
FFN_ROWS = 512
VMEM_LIMIT = 56 * 1024 * 1024


def _dot(a, b):
    return jnp.dot(a, b, preferred_element_type=F32)


def _dot_nt(a, b):
    return lax.dot_general(a, b, (((1,), (1,)), ((), ())), preferred_element_type=F32)


def _dot_tn(a, b):
    return lax.dot_general(a, b, (((0,), (0,)), ((), ())), preferred_element_type=F32)


def _bf(x):
    return x.astype(BF16)


def _sigmoid(x):
    return 1.0 / (1.0 + jnp.exp(-x))


def _norm_mod(x, gain, shift, scale):
    ms = jnp.mean(x * x, axis=-1, keepdims=True)
    return x * lax.rsqrt(ms + NORM_EPS) * gain * (1.0 + scale) + shift


def _rms(x, gain):
    ms = jnp.mean(x * x, axis=-1, keepdims=True)
    return x * lax.rsqrt(ms + NORM_EPS) * gain


def _head_ones():
    r = lax.broadcasted_iota(jnp.int32, (LANES, LANES), 0) >> HEAD_SHIFT
    c = lax.broadcasted_iota(jnp.int32, (LANES, LANES), 1) >> HEAD_SHIFT
    return jnp.where(r == c, 1.0, 0.0).astype(BF16)


def _head_sum(x, ones_bd):
    hi = _bf(x)
    lo = _bf(x - hi.astype(F32))
    out = []
    for p in range(x.shape[1] // LANES):
        sl = slice(p * LANES, (p + 1) * LANES)
        out.append(_dot(hi[:, sl], ones_bd) + _dot(lo[:, sl], ones_bd))
    return jnp.concatenate(out, axis=1)


def _shift_rows_down(x, first_row):
    y = pltpu.roll(x, 1, axis=0)
    row = lax.broadcasted_iota(jnp.int32, (8, x.shape[1]), 0)
    head = jnp.where(row == 0, first_row, y[:8])
    return jnp.concatenate([head, y[8:]], axis=0)


def _shift_rows_up(x, last_row):
    n = x.shape[0]
    y = pltpu.roll(x, n - 1, axis=0)
    row = lax.broadcasted_iota(jnp.int32, (8, x.shape[1]), 0)
    tail = jnp.where(row == 7, last_row, y[n - 8:])
    return jnp.concatenate([y[:n - 8], tail], axis=0)


def _cparams(sem):
    return pltpu.CompilerParams(dimension_semantics=sem, vmem_limit_bytes=VMEM_LIMIT)


def _mod_kernel(c_ref, w_ref, b_ref, o_ref):
    c = c_ref[...]
    sc = c * _sigmoid(c)
    o_ref[0] = _dot(_bf(sc), _bf(w_ref[0])) + b_ref[0]


def _ada_mod(c, ada_w, ada_b):
    depth, d, n = ada_w.shape
    b = c.shape[0]
    tn = n // 4
    return pl.pallas_call(
        _mod_kernel,
        grid=(depth, n // tn),
        in_specs=[
            pl.BlockSpec((b, d), lambda l, j: (0, 0)),
            pl.BlockSpec((1, d, tn), lambda l, j: (l, 0, j)),
            pl.BlockSpec((1, 1, tn), lambda l, j: (l, 0, j)),
        ],
        out_specs=pl.BlockSpec((1, b, tn), lambda l, j: (l, 0, j)),
        out_shape=jax.ShapeDtypeStruct((depth, b, n), F32),
        compiler_params=_cparams(("arbitrary", "arbitrary")),
        name="ada_mod",
    )(c, ada_w, ada_b.reshape(depth, 1, n))


def _softplus(z):
    return jnp.maximum(z, 0.0) + jnp.log(1.0 + jnp.exp(-jnp.abs(z)))


def _rw_proj_kernel(has_vres, x_ref, xp_ref, xn_ref, mod_ref, ng_ref, mu_ref, wrkv_ref, w0_ref, wA_ref, wB_ref,
                    a0_ref, aA_ref, aB_ref, gA_ref, gB_ref, kk_ref, ka_ref, rk_ref, *rest):
    if has_vres:
        vf_ref, v0_ref, vA_ref, vB_ref = rest[:4]
        rest = rest[4:]
    r_o, k_o, v_o, kkn_o, g_o, bonus_o, lw0_o, lw1_o, as0_o, as1_o = rest
    i = pl.program_id(1)
    nt = pl.num_programs(1)
    mod = mod_ref[0]
    shift, scale = mod[0:1], mod[1:2]
    gain = ng_ref[...]
    h = _norm_mod(x_ref[0], gain, shift, scale)
    h_before = _norm_mod(xp_ref[0], gain, shift, scale)[7:8]
    h_after = _norm_mod(xn_ref[0], gain, shift, scale)[0:1]
    h_before = jnp.where(i == 0, 0.0, h_before)
    h_after = jnp.where(i == nt - 1, 0.0, h_after)
    xx = 0.5 * (_shift_rows_down(h, h_before) + _shift_rows_up(h, h_after)) - h
    mu = mu_ref[...]
    xr, xw, xk, xv, xa, xg = (_bf(h + xx * mu[n:n + 1]) for n in range(6))

    r = _dot(xr, wrkv_ref[0])
    k = _dot(xk, wrkv_ref[1])
    v = _dot(xv, wrkv_ref[2])
    if has_vres:
        mix = _sigmoid(v0_ref[...] + _dot(_bf(_dot(xv, vA_ref[...])), vB_ref[...]))
        v = v + (vf_ref[0] - v) * mix
    r_o[0] = r
    k_o[0] = k
    v_o[0] = v

    g_o[0] = _dot(_bf(_sigmoid(_dot(xg, gA_ref[...]))), gB_ref[...])

    ones_bd = _head_ones()
    kk = k * kk_ref[...]
    kk_norm = jnp.sqrt(_head_sum(kk * kk, ones_bd))
    kkn_o[0] = kk / jnp.maximum(kk_norm, 1e-12)

    ka = ka_ref[...]
    kd_sum = jnp.zeros_like(k)
    for n, (lw_o, as_o) in enumerate(((lw0_o, as0_o), (lw1_o, as1_o))):
        wl = w0_ref[n:n + 1] + _dot(_bf(jnp.tanh(_dot(xw, wA_ref[n]))), wB_ref[n])
        w_log = -_softplus(-wl) - 0.5
        lw_o[0] = -jnp.exp(w_log)
        a_sig = _sigmoid(a0_ref[n:n + 1] + _dot(_bf(_dot(xa, aA_ref[n])), aB_ref[n]))
        as_o[0] = a_sig
        kd_sum = kd_sum + k * (1.0 + (a_sig - 1.0) * ka)
    bonus_o[0] = _head_sum(r * kd_sum * rk_ref[...], ones_bd) * v


def _rw_proj(x, mod, ng, mu, w_rkv, w0, wA, wB, a0, aA, aB, gA, gB, k_k, k_a, r_k, vres):
    b, t, d = x.shape
    tq = min(ROW_TILE, t)
    nt = t // tq
    has_vres = vres is not None
    row = pl.BlockSpec((1, tq, d), lambda bi, i: (bi, i, 0))

    def full(a):
        nd = a.ndim
        return pl.BlockSpec(a.shape, lambda bi, i: (0,) * nd)

    halo = tq // 8
    n8 = t // 8
    in_specs = [
        row,
        pl.BlockSpec((1, 8, d), lambda bi, i: (bi, jnp.maximum(i * halo - 1, 0), 0)),
        pl.BlockSpec((1, 8, d), lambda bi, i: (bi, jnp.minimum((i + 1) * halo, n8 - 1), 0)),
        pl.BlockSpec((1, N_MOD, d), lambda bi, i: (bi, 0, 0)),
    ]
    params = [ng, mu, w_rkv, w0, wA, wB, a0, aA, aB, gA, gB, k_k, k_a, r_k]
    args = [x, x, x, mod] + params
    in_specs += [full(a) for a in params]
    if has_vres:
        v_first, v0, vA, vB = vres
        args += [v_first, v0, vA, vB]
        in_specs += [row, full(v0), full(vA), full(vB)]
    n_out = 10
    return pl.pallas_call(
        functools.partial(_rw_proj_kernel, has_vres),
        grid=(b, nt),
        in_specs=in_specs,
        out_specs=[row] * n_out,
        out_shape=[jax.ShapeDtypeStruct((b, t, d), F32)] * n_out,
        compiler_params=_cparams(("arbitrary", "arbitrary")),
        name="rw_proj",
    )(*args)


def _wkv_kernel(reverse, r_ref, k_ref, v_ref, kk_ref, lw_ref, as_ref, ka_ref, tri_ref, y_ref, s_ref):
    L = WKV_CHUNK
    rows, d = lw_ref.shape[1], lw_ref.shape[2]
    n_chunk = rows // L
    n_tile = d // LANES

    @pl.when(pl.program_id(1) == 0)
    def _():
        s_ref[...] = jnp.zeros_like(s_ref)

    lw = lw_ref[0]
    hi = _bf(lw)
    rem = lw - hi.astype(F32)
    mid = _bf(rem)
    lo = _bf(rem - mid.astype(F32))
    tri = tri_ref[...]
    cs = _dot(tri, hi) + _dot(tri, mid) + _dot(tri, lo)

    w_in = jnp.exp(cs)
    w_inv = jnp.exp(-cs)
    w_ex = w_in * jnp.exp(-lw)
    ends = [c * L if reverse else (c + 1) * L - 1 for c in range(n_chunk)]
    dec = [jnp.exp(cs[e:e + 1]) for e in ends]
    kk = kk_ref[0]
    a_sig = as_ref[0]
    v = v_ref[0]
    a_t = -(kk * w_ex)
    r_t = r_ref[0] * w_in
    b_t = kk * a_sig * w_inv
    k_t = k_ref[0] * (1.0 + (a_sig - 1.0) * ka_ref[...]) * w_inv

    lane = lax.broadcasted_iota(jnp.int32, (L, LANES), 1)
    row = lax.broadcasted_iota(jnp.int32, (L, LANES), 0)
    col = lane & (L - 1)
    head0 = lane < RW_HEAD
    if reverse:
        strict, incl = row < col, row <= col
    else:
        strict, incl = row > col, row >= col
    eye2 = jnp.where(row == col, 1.0, 0.0)
    r2 = lax.broadcasted_iota(jnp.int32, (LANES, LANES), 0) >> HEAD_SHIFT
    c2 = lax.broadcasted_iota(jnp.int32, (LANES, LANES), 1) >> HEAD_SHIFT
    same_head = r2 == c2

    def bdiag(z):
        return _bf(jnp.concatenate([jnp.where(head0, z, 0.0), jnp.where(head0, 0.0, z)], axis=0))

    chunks = list(range(n_chunk - 1, -1, -1) if reverse else range(n_chunk))
    pairs = [(c, p) for c in chunks for p in range(n_tile)]

    def cut(z):
        return [z[c * L:(c + 1) * L, p * LANES:(p + 1) * LANES] for c, p in pairs]

    at, rt, bt, kt, vv = (cut(z) for z in (a_t, r_t, b_t, k_t, v))
    n = len(pairs)
    idx = range(n)
    dec_t = [dec[c][:, p * LANES:(p + 1) * LANES] for c, p in pairs]
    bh = [bt[i] * dec_t[i] for i in idx]
    kh = [kt[i] * dec_t[i] for i in idx]

    def dots(lhs, w):
        return [_dot(lhs[i], w[i]) for i in idx]

    vbd = [bdiag(vv[i]) for i in idx]
    gram = [_dot(_bf(jnp.concatenate([at[i], rt[i]], axis=0)),
                 jnp.concatenate([bdiag(bt[i]).T, bdiag(kt[i]).T], axis=1)) for i in idx]
    a_ab = [jnp.where(strict, gram[i][:L, :LANES], 0.0) for i in idx]
    a_ak = [jnp.where(strict, gram[i][:L, LANES:], 0.0) for i in idx]
    a_rb = [_bf(jnp.where(incl, gram[i][L:, :LANES], 0.0)) for i in idx]
    a_rk = [jnp.where(incl, gram[i][L:, LANES:], 0.0) for i in idx]
    av = dots([_bf(jnp.concatenate([a_ak[i], a_rk[i]], axis=0)) for i in idx], vbd)
    akv = [av[i][:L] for i in idx]
    y0 = [av[i][L:] for i in idx]
    x_inv = [eye2 + jnp.where((row >> 1) == (col >> 1), a_ab[i], 0.0) for i in idx]
    lev = 1
    while (1 << lev) < L:
        sel = ((row >> (lev + 1)) == (col >> (lev + 1))) & ((row >> lev) != (col >> lev))
        step = dots([_bf(x_inv[i]) for i in idx], [bdiag(jnp.where(sel, a_ab[i], 0.0)) for i in idx])
        grow = dots([_bf(step[i]) for i in idx], [bdiag(x_inv[i]) for i in idx])
        x_inv = [x_inv[i] + grow[i] for i in idx]
        lev += 1
    sol = [_dot(_bf(x_inv[i]), jnp.concatenate([bdiag(at[i]), bdiag(akv[i])], axis=1)) for i in idx]
    at2 = [sol[i][:, :LANES] for i in idx]
    u0 = [sol[i][:, LANES:] for i in idx]
    mix = [_dot(a_rb[i], jnp.concatenate([bdiag(at2[i]), bdiag(u0[i])], axis=1)) for i in idx]
    r2_ = [_bf(rt[i] + mix[i][:, :LANES]) for i in idx]
    y0 = [y0[i] + mix[i][:, LANES:] for i in idx]
    p_bd = [_bf(jnp.where(same_head, _dot_tn(_bf(at2[i]), _bf(bh[i])), 0.0)) for i in idx]
    q_bd = [jnp.where(same_head,
                      _dot_tn(_bf(jnp.concatenate([u0[i], vv[i]], axis=0)),
                              _bf(jnp.concatenate([bh[i], kh[i]], axis=0))), 0.0) for i in idx]
    state = [s_ref[p] for p in range(n_tile)]
    for i, (c, p) in enumerate(pairs):
        s0 = state[p]
        s0b = _bf(s0)
        y_ref[0, c * L:(c + 1) * L, p * LANES:(p + 1) * LANES] = _dot(r2_[i], s0b.T) + y0[i]
        state[p] = s0 * dec_t[i] + _dot(s0b, p_bd[i]) + q_bd[i]
    for p in range(n_tile):
        s_ref[p] = state[p]


def _wkv_tri(rows, reverse):
    t = jnp.arange(rows)
    same = (t[:, None] // WKV_CHUNK) == (t[None, :] // WKV_CHUNK)
    tri = (t[:, None] <= t[None, :]) if reverse else (t[:, None] >= t[None, :])
    return (same & tri).astype(BF16)


def _wkv(r, k, v, kk, lw, a_sig, k_a, reverse):
    b, t, d = r.shape
    rows = min(WKV_BLOCK, t)
    nb = t // rows
    if reverse:
        blk = pl.BlockSpec((1, rows, d), lambda bi, i: (bi, nb - 1 - i, 0))
    else:
        blk = pl.BlockSpec((1, rows, d), lambda bi, i: (bi, i, 0))
    return pl.pallas_call(
        functools.partial(_wkv_kernel, reverse),
        grid=(b, nb),
        in_specs=[blk] * 6 + [
            pl.BlockSpec((1, d), lambda bi, i: (0, 0)),
            pl.BlockSpec((rows, rows), lambda bi, i: (0, 0)),
        ],
        out_specs=blk,
        out_shape=jax.ShapeDtypeStruct((b, t, d), F32),
        scratch_shapes=[pltpu.VMEM((d // LANES, LANES, LANES), F32)],
        compiler_params=_cparams(("arbitrary", "arbitrary")),
        name="wkv_bwd" if reverse else "wkv_fwd",
    )(r, k, v, kk, lw, a_sig, k_a, _wkv_tri(rows, reverse))


def _rw_out_kernel(yf_ref, yb_ref, bonus_ref, g_ref, x_ref, mod_ref, ng_ref, lnx_ref, wo_ref, o_ref):
    ones_bd = _head_ones()
    y = yf_ref[0] + yb_ref[0]
    inv_n = 1.0 / RW_HEAD
    mean = _head_sum(y, ones_bd) * inv_n
    yc = y - mean
    var = _head_sum(yc * yc, ones_bd) * inv_n
    lnx = lnx_ref[...]
    yn = yc * lax.rsqrt(var + GN_EPS) * lnx[0:1] + lnx[1:2]
    o = (yn + bonus_ref[0]) * g_ref[0]
    out = _dot(_bf(o), wo_ref[...])
    gate = mod_ref[0][2:3]
    o_ref[0] = x_ref[0] + gate * _rms(out, ng_ref[...])


def _rw_out(y_f, y_b, bonus, g, x, mod, ng, lnx, w_o):
    b, t, d = x.shape
    tq = min(ROW_TILE, t)
    row = pl.BlockSpec((1, tq, d), lambda bi, i: (bi, i, 0))
    return pl.pallas_call(
        _rw_out_kernel,
        grid=(b, t // tq),
        in_specs=[row] * 5 + [
            pl.BlockSpec((1, N_MOD, d), lambda bi, i: (bi, 0, 0)),
            pl.BlockSpec((1, d), lambda bi, i: (0, 0)),
            pl.BlockSpec((2, d), lambda bi, i: (0, 0)),
            pl.BlockSpec((d, d), lambda bi, i: (0, 0)),
        ],
        out_specs=row,
        out_shape=jax.ShapeDtypeStruct((b, t, d), F32),
        compiler_params=_cparams(("arbitrary", "arbitrary")),
        name="rw_out",
    )(y_f, y_b, bonus, g, x, mod, ng, lnx, w_o)


def _rope(x, cos, sin_signed):
    lane = lax.broadcasted_iota(jnp.int32, x.shape, 1)
    partner = jnp.where((lane & 1) == 0, pltpu.roll(x, LANES - 1, axis=1), pltpu.roll(x, 1, axis=1))
    return x * cos + partner * sin_signed


def _at_qkv_kernel(x_ref, mod_ref, ng_ref, w_ref, qkg_ref, cos_ref, sin_ref, q_o, k_o, vt_o):
    mod = mod_ref[0]
    h = _norm_mod(x_ref[0], ng_ref[...], mod[0:1], mod[1:2])
    qkv = _dot(_bf(h), w_ref[...])
    nq = q_o.shape[2]
    nk = k_o.shape[2]
    cos, sin = cos_ref[...], sin_ref[...]
    qkg = qkg_ref[...]
    scale = AT_HEAD ** -0.5

    def head(z, gain, mul):
        ms = jnp.mean(z * z, axis=-1, keepdims=True)
        return _rope(z * lax.rsqrt(ms + NORM_EPS) * gain, cos, sin) * mul

    for n in range(nq // AT_HEAD):
        sl = slice(n * AT_HEAD, (n + 1) * AT_HEAD)
        q_o[0, :, sl] = _bf(head(qkv[:, sl], qkg[0:1], scale))
    for n in range(nk // AT_HEAD):
        sl = slice(n * AT_HEAD, (n + 1) * AT_HEAD)
        k_o[0, :, sl] = _bf(head(qkv[:, nq + n * AT_HEAD:nq + (n + 1) * AT_HEAD], qkg[1:2], 1.0))
    vt_o[0] = _bf(qkv[:, nq + nk:].T)


def _at_qkv(x, mod, ng, w_qkv, qk_g, cos, sin):
    b, t, d = x.shape
    tq = min(ROW_TILE, t)
    nk = AT_KVH * AT_HEAD
    row = lambda n: pl.BlockSpec((1, tq, n), lambda bi, i: (bi, i, 0))
    return pl.pallas_call(
        _at_qkv_kernel,
        grid=(b, t // tq),
        in_specs=[
            row(d),
            pl.BlockSpec((1, N_MOD, d), lambda bi, i: (bi, 0, 0)),
            pl.BlockSpec((1, d), lambda bi, i: (0, 0)),
            pl.BlockSpec(w_qkv.shape, lambda bi, i: (0, 0)),
            pl.BlockSpec((2, AT_HEAD), lambda bi, i: (0, 0)),
            pl.BlockSpec((tq, AT_HEAD), lambda bi, i: (i, 0)),
            pl.BlockSpec((tq, AT_HEAD), lambda bi, i: (i, 0)),
        ],
        out_specs=[row(d), row(nk), pl.BlockSpec((1, nk, tq), lambda bi, i: (bi, 0, i))],
        out_shape=[jax.ShapeDtypeStruct((b, t, d), BF16), jax.ShapeDtypeStruct((b, t, nk), BF16),
                   jax.ShapeDtypeStruct((b, nk, t), BF16)],
        compiler_params=_cparams(("arbitrary", "arbitrary")),
        name="at_qkv",
    )(x, mod, ng, w_qkv, qk_g, cos, sin)


def _attn_kernel(q_ref, k_ref, vt_ref, o_ref):
    group = q_ref.shape[2] // AT_HEAD
    k = k_ref[0]
    vt = vt_ref[0]
    def scores(g):
        return _dot_nt(k, q_ref[0, :, g * AT_HEAD:(g + 1) * AT_HEAD])

    st = scores(0)
    for g in range(group):
        st_next = scores(g + 1) if g + 1 < group else None
        p = jnp.exp(st - jnp.max(st, axis=0, keepdims=True))
        denom = jnp.sum(p, axis=0, keepdims=True)
        ot = _dot(vt, _bf(p)) / denom
        o_ref[0, :, g * AT_HEAD:(g + 1) * AT_HEAD] = _bf(ot.T)
        st = st_next


def _attention(q, k, vt):
    b, t, d = q.shape
    group = d // AT_HEAD // AT_KVH
    gw = group * AT_HEAD
    tq = min(ATT_Q_TILE, t)
    qblk = pl.BlockSpec((1, tq, gw), lambda bi, h, i: (bi, i, h))
    return pl.pallas_call(
        _attn_kernel,
        grid=(b, AT_KVH, t // tq),
        in_specs=[qblk,
                  pl.BlockSpec((1, t, AT_HEAD), lambda bi, h, i: (bi, 0, h)),
                  pl.BlockSpec((1, AT_HEAD, t), lambda bi, h, i: (bi, h, 0))],
        out_specs=qblk,
        out_shape=jax.ShapeDtypeStruct((b, t, d), BF16),
        compiler_params=_cparams(("arbitrary", "arbitrary", "arbitrary")),
        name="attention",
    )(q, k, vt)


def _at_out_kernel(a_ref, x_ref, mod_ref, ng_ref, wo_ref, o_ref):
    out = _dot(a_ref[0], wo_ref[...])
    gate = mod_ref[0][2:3]
    o_ref[0] = x_ref[0] + gate * _rms(out, ng_ref[...])


def _at_out(a, x, mod, ng, w_o):
    b, t, d = x.shape
    tq = min(ROW_TILE, t)
    row = pl.BlockSpec((1, tq, d), lambda bi, i: (bi, i, 0))
    return pl.pallas_call(
        _at_out_kernel,
        grid=(b, t // tq),
        in_specs=[row, row,
                  pl.BlockSpec((1, N_MOD, d), lambda bi, i: (bi, 0, 0)),
                  pl.BlockSpec((1, d), lambda bi, i: (0, 0)),
                  pl.BlockSpec((d, d), lambda bi, i: (0, 0))],
        out_specs=row,
        out_shape=jax.ShapeDtypeStruct((b, t, d), F32),
        compiler_params=_cparams(("arbitrary", "arbitrary")),
        name="at_out",
    )(a, x, mod, ng, w_o)


def _ffn_kernel(x_ref, mod_ref, ng_ref, wu_ref, wg_ref, cw_ref, cb_ref, wd_ref, o_ref, h_ref):
    f = pl.program_id(1)
    mod = mod_ref[0]

    @pl.when(f == 0)
    def _():
        ng = ng_ref[...]
        h_ref[...] = _bf(_norm_mod(x_ref[0], ng[0:1], mod[3:4], mod[4:5]))
        o_ref[...] = jnp.zeros_like(o_ref)

    t = h_ref.shape[0]
    rows = min(FFN_ROWS, t)
    n_blk = t // rows
    cw = cw_ref[...]
    cb = cb_ref[...]
    zero = jnp.zeros((1, cw.shape[1]), F32)

    def up(b):
        hb = h_ref[b * rows:(b + 1) * rows]
        return _dot(hb, wu_ref[0]), _dot(hb, wg_ref[0])

    ug = [up(0)] + [None] * (n_blk - 1)
    for b in range(n_blk):
        if b + 1 < n_blk:
            ug[b + 1] = up(b + 1)
        u, gv = ug[b]
        before = ug[b - 1][0][rows - 1:rows] if b > 0 else zero
        after = ug[b + 1][0][0:1] if b + 1 < n_blk else zero
        u = cw[0:1] * _shift_rows_down(u, before) + cw[1:2] * u + cw[2:3] * _shift_rows_up(u, after) + cb
        act = u * _sigmoid(u) * gv
        o_ref[0, b * rows:(b + 1) * rows] += _dot(_bf(act), wd_ref[...])

    @pl.when(f == pl.num_programs(1) - 1)
    def _():
        ng = ng_ref[...]
        o_ref[0] = x_ref[0] + mod[5:6] * _rms(o_ref[0], ng[1:2])


def _ffn(x, mod, ng, w_up, conv_w, conv_b, w_down):
    b, t, d = x.shape
    f_dim = w_down.shape[0]
    ft = min(FFN_TILE, f_dim)
    nf = f_dim // ft
    seq = pl.BlockSpec((1, t, d), lambda bi, f: (bi, 0, 0))
    return pl.pallas_call(
        _ffn_kernel,
        grid=(b, nf),
        in_specs=[
            seq,
            pl.BlockSpec((1, N_MOD, d), lambda bi, f: (bi, 0, 0)),
            pl.BlockSpec((2, d), lambda bi, f: (0, 0)),
            pl.BlockSpec((1, d, ft), lambda bi, f: (0, 0, f)),
            pl.BlockSpec((1, d, ft), lambda bi, f: (1, 0, f)),
            pl.BlockSpec((3, ft), lambda bi, f: (0, f)),
            pl.BlockSpec((1, ft), lambda bi, f: (0, f)),
            pl.BlockSpec((ft, d), lambda bi, f: (f, 0)),
        ],
        out_specs=seq,
        out_shape=jax.ShapeDtypeStruct((b, t, d), F32),
        scratch_shapes=[pltpu.VMEM((t, d), BF16)],
        compiler_params=_cparams(("arbitrary", "arbitrary")),
        name="ffn",
    )(x, mod, ng, w_up, w_up, conv_w, conv_b.reshape(1, f_dim), w_down)


def _rope_tables(t):
    pos = jnp.arange(t)
    row = (pos // GRID_W).astype(F32)
    col = (pos % GRID_W).astype(F32)
    n_pair = AT_HEAD // 4
    inv = ROPE_THETA ** (-jnp.arange(n_pair, dtype=F32) / n_pair)
    ang = jnp.concatenate([row[:, None] * inv, col[:, None] * inv], axis=-1)
    cos = jnp.repeat(jnp.cos(ang), 2, axis=-1)
    sin = jnp.repeat(jnp.sin(ang), 2, axis=-1)
    sign = jnp.where(jnp.arange(AT_HEAD) % 2 == 0, -1.0, 1.0)
    return cos, sin * sign


def kernel(x_prompt, x_sample, c_prompt, c_sample, ada_w, ada_b, norm_g, rw_mu, rw_rkv, rw_w0, rw_wA, rw_wB, rw_a0,
           rw_aA, rw_aB, rw_gA, rw_gB, rw_kk, rw_ka, rw_rk, rw_lnx, rw_o, rw_v0, rw_vA, rw_vB, at_qkv, at_qk_g,
           at_o, ffn_up, ffn_conv_w, ffn_conv_b, ffn_down):
    assert x_prompt.shape[1:] == x_sample.shape[1:]
    n_prompt = x_prompt.shape[0]
    x = jnp.concatenate([x_prompt, x_sample], axis=0)
    c = jnp.concatenate([c_prompt, c_sample], axis=0)
    b, t, d = x.shape
    depth = ada_w.shape[0]
    f_dim = ffn_down.shape[1]

    mod_all = _ada_mod(c, ada_w, ada_b).reshape(depth, b, N_MOD, d)
    cos, sin = _rope_tables(t)
    v_first = None
    for l in range(depth):
        mod = mod_all[l]
        ng = norm_g[l]
        j = l // 2
        if l % 2 == 0:
            vres = None
            if j > 0:
                vres = (v_first, rw_v0[j - 1].reshape(1, d), _bf(rw_vA[j - 1]), _bf(rw_vB[j - 1]))
            r, k, v, kk, g, bonus, lw0, lw1, as0, as1 = _rw_proj(
                x, mod, ng[0:1], rw_mu[j], _bf(rw_rkv[j]), rw_w0[j], _bf(rw_wA[j]), _bf(rw_wB[j]), rw_a0[j],
                _bf(rw_aA[j]), _bf(rw_aB[j]), _bf(rw_gA[j]), _bf(rw_gB[j]), rw_kk[j].reshape(1, d),
                rw_ka[j].reshape(1, d), rw_rk[j].reshape(1, d), vres)
            if j == 0:
                v_first = v
            ka = rw_ka[j].reshape(1, d)
            y_f = _wkv(r, k, v, kk, lw0, as0, ka, reverse=False)
            y_b = _wkv(r, k, v, kk, lw1, as1, ka, reverse=True)
            x = _rw_out(y_f, y_b, bonus, g, x, mod, ng[1:2], rw_lnx[j], _bf(rw_o[j]))
        else:
            q, k, v = _at_qkv(x, mod, ng[0:1], _bf(at_qkv[j]), at_qk_g[j], cos, sin)
            a = _attention(q, k, v)
            x = _at_out(a, x, mod, ng[1:2], _bf(at_o[j]))
        w_up = _bf(ffn_up[l]).reshape(d, 2, f_dim).transpose(1, 0, 2)
        x = _ffn(x, mod, ng[2:4], w_up, ffn_conv_w[l], ffn_conv_b[l], _bf(ffn_down[l]))
    return (x[:n_prompt], x[n_prompt:])
```

```python
import functools

import jax
import jax.numpy as jnp
from jax import lax
from jax.experimental import pallas as pl
from jax.experimental.pallas import tpu as pltpu

F32 = jnp.float32
BF16 = jnp.bfloat16

RW_HEAD = 64
AT_HEAD = 128
AT_KVH = 2
GRID_W = 64
ROPE_THETA = 10000.0
NORM_EPS = 1e-6
GN_EPS = 64e-5
N_MOD = 6

LOG2_E = 1.4426950408889634
LANES = 128
BF16_ROWS = 16
HEAD_SHIFT = RW_HEAD.bit_length() - 1
WKV_CHUNK = 64
WKV_BLOCK = 256
WKV_SHARE = 0
WKV_GROUP = 4
WKV_SKEW = 0
ROW_TILE = 256
ATT_Q_TILE = 256
ATT_K_BLOCK = 256
ATT_AHEAD = 8
FFN_TILE = 256
FFN_ROWS = 512
VMEM_LIMIT = 56 * 1024 * 1024


def _dot(a, b):
    return jnp.dot(a, b, preferred_element_type=F32)


def _dot_nt(a, b):
    return lax.dot_general(a, b, (((1,), (1,)), ((), ())), preferred_element_type=F32)


def _dot_tn(a, b):
    return lax.dot_general(a, b, (((0,), (0,)), ((), ())), preferred_element_type=F32)


def _bf(x):
    return x.astype(BF16)


def _sigmoid(x):
    return 1.0 / (1.0 + jnp.exp(-x))


def _norm_mod(x, gain, shift, scale):
    ms = jnp.mean(x * x, axis=-1, keepdims=True)
    return x * lax.rsqrt(ms + NORM_EPS) * gain * (1.0 + scale) + shift


def _rms(x, gain):
    ms = jnp.mean(x * x, axis=-1, keepdims=True)
    return x * lax.rsqrt(ms + NORM_EPS) * gain


def _head_ones():
    r = lax.broadcasted_iota(jnp.int32, (LANES, LANES), 0) >> HEAD_SHIFT
    c = lax.broadcasted_iota(jnp.int32, (LANES, LANES), 1) >> HEAD_SHIFT
    return jnp.where(r == c, 1.0, 0.0).astype(BF16)


def _head_sum(x, ones_bd):
    hi = _bf(x)
    lo = _bf(x - hi.astype(F32))
    out = []
    for p in range(x.shape[1] // LANES):
        sl = slice(p * LANES, (p + 1) * LANES)
        out.append(_dot(hi[:, sl], ones_bd) + _dot(lo[:, sl], ones_bd))
    return jnp.concatenate(out, axis=1)


def _shift_rows_down(x, first_row):
    y = pltpu.roll(x, 1, axis=0)
    row = lax.broadcasted_iota(jnp.int32, (8, x.shape[1]), 0)
    head = jnp.where(row == 0, first_row, y[:8])
    return jnp.concatenate([head, y[8:]], axis=0)


def _shift_rows_up(x, last_row):
    n = x.shape[0]
    y = pltpu.roll(x, n - 1, axis=0)
    row = lax.broadcasted_iota(jnp.int32, (8, x.shape[1]), 0)
    tail = jnp.where(row == 7, last_row, y[n - 8:])
    return jnp.concatenate([y[:n - 8], tail], axis=0)


def _cparams(sem):
    return pltpu.CompilerParams(dimension_semantics=sem, vmem_limit_bytes=VMEM_LIMIT)


def _mod_kernel(c_ref, w_ref, b_ref, o_ref):
    c = c_ref[...]
    sc = c * _sigmoid(c)
    o_ref[0] = _dot(_bf(sc), _bf(w_ref[0])) + b_ref[0]


def _ada_mod(c, ada_w, ada_b):
    depth, d, n = ada_w.shape
    b = c.shape[0]
    tn = n // 4
    return pl.pallas_call(
        _mod_kernel,
        grid=(depth, n // tn),
        in_specs=[
            pl.BlockSpec((b, d), lambda l, j: (0, 0)),
            pl.BlockSpec((1, d, tn), lambda l, j: (l, 0, j)),
            pl.BlockSpec((1, 1, tn), lambda l, j: (l, 0, j)),
        ],
        out_specs=pl.BlockSpec((1, b, tn), lambda l, j: (l, 0, j)),
        out_shape=jax.ShapeDtypeStruct((depth, b, n), F32),
        compiler_params=_cparams(("arbitrary", "arbitrary")),
        name="ada_mod",
    )(c, ada_w, ada_b.reshape(depth, 1, n))


def _softplus(z):
    return jnp.maximum(z, 0.0) + jnp.log(1.0 + jnp.exp(-jnp.abs(z)))


def _rw_proj_kernel(has_vres, x_ref, xp_ref, xn_ref, mod_ref, ng_ref, mu_ref, wrkv_ref, w0_ref, wA_ref, wB_ref,
                    a0_ref, aA_ref, aB_ref, gA_ref, gB_ref, kk_ref, ka_ref, rk_ref, *rest):
    if has_vres:
        vf_ref, v0_ref, vA_ref, vB_ref = rest[:4]
        rest = rest[4:]
    r_o, k_o, v_o, kkn_o, g_o, bonus_o, lw0_o, lw1_o, as0_o, as1_o = rest
    i = pl.program_id(1)
    nt = pl.num_programs(1)
    mod = mod_ref[0]
    shift, scale = mod[0:1], mod[1:2]
    gain = ng_ref[...]
    h = _norm_mod(x_ref[0], gain, shift, scale)
    h_before = _norm_mod(xp_ref[0], gain, shift, scale)[7:8]
    h_after = _norm_mod(xn_ref[0], gain, shift, scale)[0:1]
    h_before = jnp.where(i == 0, 0.0, h_before)
    h_after = jnp.where(i == nt - 1, 0.0, h_after)
    xx = 0.5 * (_shift_rows_down(h, h_before) + _shift_rows_up(h, h_after)) - h
    mu = mu_ref[...]
    xr, xw, xk, xv, xa, xg = (_bf(h + xx * mu[n:n + 1]) for n in range(6))

    r = _dot(xr, wrkv_ref[0])
    k = _dot(xk, wrkv_ref[1])
    v = _dot(xv, wrkv_ref[2])
    if has_vres:
        mix = _sigmoid(v0_ref[...] + _dot(_bf(_dot(xv, vA_ref[...])), vB_ref[...]))
        v = v + (vf_ref[0] - v) * mix
    r_o[0] = r
    k_o[0] = k
    v_o[0] = v

    g_o[0] = _dot(_bf(_sigmoid(_dot(xg, gA_ref[...]))), gB_ref[...])

    ones_bd = _head_ones()
    kk = k * kk_ref[...]
    kk_norm = jnp.sqrt(_head_sum(kk * kk, ones_bd))
    kkn_o[0] = kk / jnp.maximum(kk_norm, 1e-12)

    ka = ka_ref[...]
    kd_sum = jnp.zeros_like(k)
    for n, (lw_o, as_o) in enumerate(((lw0_o, as0_o), (lw1_o, as1_o))):
        wl = w0_ref[n:n + 1] + _dot(_bf(jnp.tanh(_dot(xw, wA_ref[n]))), wB_ref[n])
        w_log = -_softplus(-wl) - 0.5
        lw_o[0] = -jnp.exp(w_log)
        a_sig = _sigmoid(a0_ref[n:n + 1] + _dot(_bf(_dot(xa, aA_ref[n])), aB_ref[n]))
        as_o[0] = a_sig
        kd_sum = kd_sum + k * (1.0 + (a_sig - 1.0) * ka)
    bonus_o[0] = _head_sum(r * kd_sum * rk_ref[...], ones_bd) * v


def _rw_proj(x, mod, ng, mu, w_rkv, w0, wA, wB, a0, aA, aB, gA, gB, k_k, k_a, r_k, vres):
    b, t, d = x.shape
    tq = min(ROW_TILE, t)
    nt = t // tq
    has_vres = vres is not None
    row = pl.BlockSpec((1, tq, d), lambda bi, i: (bi, i, 0))

    def full(a):
        nd = a.ndim
        return pl.BlockSpec(a.shape, lambda bi, i: (0,) * nd)

    halo = tq // 8
    n8 = t // 8
    in_specs = [
        row,
        pl.BlockSpec((1, 8, d), lambda bi, i: (bi, jnp.maximum(i * halo - 1, 0), 0)),
        pl.BlockSpec((1, 8, d), lambda bi, i: (bi, jnp.minimum((i + 1) * halo, n8 - 1), 0)),
        pl.BlockSpec((1, N_MOD, d), lambda bi, i: (bi, 0, 0)),
    ]
    params = [ng, mu, w_rkv, w0, wA, wB, a0, aA, aB, gA, gB, k_k, k_a, r_k]
    args = [x, x, x, mod] + params
    in_specs += [full(a) for a in params]
    if has_vres:
        v_first, v0, vA, vB = vres
        args += [v_first, v0, vA, vB]
        in_specs += [row, full(v0), full(vA), full(vB)]
    n_out = 10
    return pl.pallas_call(
        functools.partial(_rw_proj_kernel, has_vres),
        grid=(b, nt),
        in_specs=in_specs,
        out_specs=[row] * n_out,
        out_shape=[jax.ShapeDtypeStruct((b, t, d), F32)] * n_out,
        compiler_params=_cparams(("arbitrary", "arbitrary")),
        name="rw_proj",
    )(*args)


def _wkv_kernel(reverse, r_ref, k_ref, v_ref, kk_ref, lw_ref, as_ref, ka_ref, tri_ref, y_ref, s_ref):
    L = WKV_CHUNK
    rows, d = lw_ref.shape[1], lw_ref.shape[2]
    n_chunk = rows // L
    n_tile = d // LANES

    @pl.when(pl.program_id(1) == 0)
    def _():
        s_ref[...] = jnp.zeros_like(s_ref)

    lw = lw_ref[0]
    hi = _bf(lw)
    rem = lw - hi.astype(F32)
    mid = _bf(rem)
    lo = _bf(rem - mid.astype(F32))
    tri = tri_ref[...]
    cs = _dot(tri, hi) + _dot(tri, mid) + _dot(tri, lo)

    w_in = jnp.exp(cs)
    w_inv = jnp.exp(-cs)
    w_ex = w_in * jnp.exp(-lw)
    ends = [c * L if reverse else (c + 1) * L - 1 for c in range(n_chunk)]
    dec = [jnp.exp(cs[e:e + 1]) for e in ends]
    kk = kk_ref[0]
    a_sig = as_ref[0]
    v = v_ref[0]
    a_t = -(kk * w_ex)
    r_t = r_ref[0] * w_in
    b_t = kk * a_sig * w_inv
    k_t = k_ref[0] * (1.0 + (a_sig - 1.0) * ka_ref[...]) * w_inv

    lane = lax.broadcasted_iota(jnp.int32, (L, LANES), 1)
    row = lax.broadcasted_iota(jnp.int32, (L, LANES), 0)
    col = lane & (L - 1)
    head0 = lane < RW_HEAD
    if reverse:
        strict, incl = row < col, row <= col
    else:
        strict, incl = row > col, row >= col
    eye2 = jnp.where(row == col, 1.0, 0.0)
    r2 = lax.broadcasted_iota(jnp.int32, (LANES, LANES), 0) >> HEAD_SHIFT
    c2 = lax.broadcasted_iota(jnp.int32, (LANES, LANES), 1) >> HEAD_SHIFT
    same_head = r2 == c2

    def bdiag(z):
        return _bf(jnp.concatenate([jnp.where(head0, z, 0.0), jnp.where(head0, 0.0, z)], axis=0))

    chunks = list(range(n_chunk - 1, -1, -1) if reverse else range(n_chunk))
    group_chunks = [chunks[g:g + WKV_GROUP] for g in range(0, n_chunk, WKV_GROUP)]
    done = {}

    def chains(g):
        pairs = [(c, p) for c in group_chunks[g] for p in range(n_tile)]
        idx = range(len(pairs))

        def cut(z):
            return [z[c * L:(c + 1) * L, p * LANES:(p + 1) * LANES] for c, p in pairs]

        def dots(lhs, w):
            m = lhs[0].shape[0]
            out = [None] * len(idx)
            n_share = min(WKV_SHARE, len(idx)) // 2 * 2
            for i in range(0, n_share, 2):
                o = _dot(jnp.concatenate([lhs[i], lhs[i + 1]], axis=0), jnp.concatenate([w[i], w[i + 1]], axis=1))
                out[i], out[i + 1] = o[:m, :LANES], o[m:, LANES:]
            for i in range(n_share, len(idx)):
                out[i] = _dot(lhs[i], w[i])
            return out

        at, rt, bt, kt, vv = (cut(z) for z in (a_t, r_t, b_t, k_t, v))
        dec_t = [dec[c][:, p * LANES:(p + 1) * LANES] for c, p in pairs]
        bh = [bt[i] * dec_t[i] for i in idx]
        kh = [kt[i] * dec_t[i] for i in idx]
        vbd = [bdiag(vv[i]) for i in idx]
        gram = [_dot(_bf(jnp.concatenate([at[i], rt[i]], axis=0)),
                     jnp.concatenate([bdiag(bt[i]).T, bdiag(kt[i]).T], axis=1)) for i in idx]
        yield
        a_ab = [jnp.where(strict, gram[i][:L, :LANES], 0.0) for i in idx]
        a_ak = [jnp.where(strict, gram[i][:L, LANES:], 0.0) for i in idx]
        a_rb = [_bf(jnp.where(incl, gram[i][L:, :LANES], 0.0)) for i in idx]
        a_rk = [jnp.where(incl, gram[i][L:, LANES:], 0.0) for i in idx]
        av = dots([_bf(jnp.concatenate([a_ak[i], a_rk[i]], axis=0)) for i in idx], vbd)
        akv = [av[i][:L] for i in idx]
        y0 = [av[i][L:] for i in idx]
        yield
        x_inv = [eye2 + jnp.where((row >> 1) == (col >> 1), a_ab[i], 0.0) for i in idx]
        lev = 1
        while (1 << lev) < L:
            sel = ((row >> (lev + 1)) == (col >> (lev + 1))) & ((row >> lev) != (col >> lev))
            step = dots([_bf(x_inv[i]) for i in idx], [bdiag(jnp.where(sel, a_ab[i], 0.0)) for i in idx])
            yield
            grow = dots([_bf(step[i]) for i in idx], [bdiag(x_inv[i]) for i in idx])
            yield
            x_inv = [x_inv[i] + grow[i] for i in idx]
            lev += 1
        sol = [_dot(_bf(x_inv[i]), jnp.concatenate([bdiag(at[i]), bdiag(akv[i])], axis=1)) for i in idx]
        yield
        at2 = [sol[i][:, :LANES] for i in idx]
        u0 = [sol[i][:, LANES:] for i in idx]
        mix = [_dot(a_rb[i], jnp.concatenate([bdiag(at2[i]), bdiag(u0[i])], axis=1)) for i in idx]
        yield
        r2_ = [_bf(rt[i] + mix[i][:, :LANES]) for i in idx]
        y0 = [y0[i] + mix[i][:, LANES:] for i in idx]
        p_bd = [_bf(jnp.where(same_head, _dot_tn(_bf(at2[i]), _bf(bh[i])), 0.0)) for i in idx]
        yield
        q_bd = [jnp.where(same_head,
                          _dot_tn(_bf(jnp.concatenate([u0[i], vv[i]], axis=0)),
                                  _bf(jnp.concatenate([bh[i], kh[i]], axis=0))), 0.0) for i in idx]
        done[g] = (pairs, r2_, y0, p_bd, q_bd, dec_t)

    def state_steps():
        state = [s_ref[p] for p in range(n_tile)]
        for g in range(len(group_chunks)):
            while g not in done:
                yield
            pairs, r2_, y0, p_bd, q_bd, dec_t = done[g]
            for i, (c, p) in enumerate(pairs):
                s0 = state[p]
                s0b = _bf(s0)
                y_ref[0, c * L:(c + 1) * L, p * LANES:(p + 1) * LANES] = _dot(r2_[i], s0b.T) + y0[i]
                state[p] = s0 * dec_t[i] + _dot(s0b, p_bd[i]) + q_bd[i]
                if p == n_tile - 1:
                    yield
        for p in range(n_tile):
            s_ref[p] = state[p]

    active = [state_steps()]
    started = 0
    tick = 0
    while active:
        if started < len(group_chunks) and tick >= started * WKV_SKEW:
            active.insert(len(active) - 1, chains(started))
            started += 1
        for gen in list(active):
            try:
                next(gen)
            except StopIteration:
                active.remove(gen)
        tick += 1


def _wkv_tri(rows, reverse):
    t = jnp.arange(rows)
    same = (t[:, None] // WKV_CHUNK) == (t[None, :] // WKV_CHUNK)
    tri = (t[:, None] <= t[None, :]) if reverse else (t[:, None] >= t[None, :])
    return (same & tri).astype(BF16)


def _wkv(r, k, v, kk, lw, a_sig, k_a, reverse):
    b, t, d = r.shape
    rows = min(WKV_BLOCK, t)
    nb = t // rows
    if reverse:
        blk = pl.BlockSpec((1, rows, d), lambda bi, i: (bi, nb - 1 - i, 0))
    else:
        blk = pl.BlockSpec((1, rows, d), lambda bi, i: (bi, i, 0))
    return pl.pallas_call(
        functools.partial(_wkv_kernel, reverse),
        grid=(b, nb),
        in_specs=[blk] * 6 + [
            pl.BlockSpec((1, d), lambda bi, i: (0, 0)),
            pl.BlockSpec((rows, rows), lambda bi, i: (0, 0)),
        ],
        out_specs=blk,
        out_shape=jax.ShapeDtypeStruct((b, t, d), F32),
        scratch_shapes=[pltpu.VMEM((d // LANES, LANES, LANES), F32)],
        compiler_params=_cparams(("arbitrary", "arbitrary")),
        name="wkv_bwd" if reverse else "wkv_fwd",
    )(r, k, v, kk, lw, a_sig, k_a, _wkv_tri(rows, reverse))


def _rw_out_kernel(yf_ref, yb_ref, bonus_ref, g_ref, x_ref, mod_ref, ng_ref, lnx_ref, wo_ref, o_ref):
    ones_bd = _head_ones()
    y = yf_ref[0] + yb_ref[0]
    inv_n = 1.0 / RW_HEAD
    mean = _head_sum(y, ones_bd) * inv_n
    yc = y - mean
    var = _head_sum(yc * yc, ones_bd) * inv_n
    lnx = lnx_ref[...]
    yn = yc * lax.rsqrt(var + GN_EPS) * lnx[0:1] + lnx[1:2]
    o = (yn + bonus_ref[0]) * g_ref[0]
    out = _dot(_bf(o), wo_ref[...])
    gate = mod_ref[0][2:3]
    o_ref[0] = x_ref[0] + gate * _rms(out, ng_ref[...])


def _rw_out(y_f, y_b, bonus, g, x, mod, ng, lnx, w_o):
    b, t, d = x.shape
    tq = min(ROW_TILE, t)
    row = pl.BlockSpec((1, tq, d), lambda bi, i: (bi, i, 0))
    return pl.pallas_call(
        _rw_out_kernel,
        grid=(b, t // tq),
        in_specs=[row] * 5 + [
            pl.BlockSpec((1, N_MOD, d), lambda bi, i: (bi, 0, 0)),
            pl.BlockSpec((1, d), lambda bi, i: (0, 0)),
            pl.BlockSpec((2, d), lambda bi, i: (0, 0)),
            pl.BlockSpec((d, d), lambda bi, i: (0, 0)),
        ],
        out_specs=row,
        out_shape=jax.ShapeDtypeStruct((b, t, d), F32),
        compiler_params=_cparams(("arbitrary", "arbitrary")),
        name="rw_out",
    )(y_f, y_b, bonus, g, x, mod, ng, lnx, w_o)


def _rope(x, cos, sin_signed):
    lane = lax.broadcasted_iota(jnp.int32, x.shape, 1)
    partner = jnp.where((lane & 1) == 0, pltpu.roll(x, LANES - 1, axis=1), pltpu.roll(x, 1, axis=1))
    return x * cos + partner * sin_signed


def _at_qkv_kernel(x_ref, mod_ref, ng_ref, w_ref, qkg_ref, cos_ref, sin_ref, q_o, k_o, vt_o):
    mod = mod_ref[0]
    h = _norm_mod(x_ref[0], ng_ref[...], mod[0:1], mod[1:2])
    qkv = _dot(_bf(h), w_ref[...])
    nq = q_o.shape[2]
    nk = k_o.shape[2]
    cos, sin = cos_ref[...], sin_ref[...]
    qkg = qkg_ref[...]
    scale = AT_HEAD ** -0.5 * LOG2_E

    def head(z, gain, mul):
        ms = jnp.mean(z * z, axis=-1, keepdims=True)
        return _rope(z * lax.rsqrt(ms + NORM_EPS) * gain, cos, sin) * mul

    for n in range(nq // AT_HEAD):
        sl = slice(n * AT_HEAD, (n + 1) * AT_HEAD)
        q_o[0, :, sl] = _bf(head(qkv[:, sl], qkg[0:1], scale))
    for n in range(nk // AT_HEAD):
        sl = slice(n * AT_HEAD, (n + 1) * AT_HEAD)
        k_o[0, :, sl] = _bf(head(qkv[:, nq + n * AT_HEAD:nq + (n + 1) * AT_HEAD], qkg[1:2], 1.0))
    vt_o[0] = _bf(qkv[:, nq + nk:].T)


def _at_qkv(x, mod, ng, w_qkv, qk_g, cos, sin):
    b, t, d = x.shape
    tq = min(ROW_TILE, t)
    nk = AT_KVH * AT_HEAD
    row = lambda n: pl.BlockSpec((1, tq, n), lambda bi, i: (bi, i, 0))
    return pl.pallas_call(
        _at_qkv_kernel,
        grid=(b, t // tq),
        in_specs=[
            row(d),
            pl.BlockSpec((1, N_MOD, d), lambda bi, i: (bi, 0, 0)),
            pl.BlockSpec((1, d), lambda bi, i: (0, 0)),
            pl.BlockSpec(w_qkv.shape, lambda bi, i: (0, 0)),
            pl.BlockSpec((2, AT_HEAD), lambda bi, i: (0, 0)),
            pl.BlockSpec((tq, AT_HEAD), lambda bi, i: (i, 0)),
            pl.BlockSpec((tq, AT_HEAD), lambda bi, i: (i, 0)),
        ],
        out_specs=[row(d), row(nk), pl.BlockSpec((1, nk, tq), lambda bi, i: (bi, 0, i))],
        out_shape=[jax.ShapeDtypeStruct((b, t, d), BF16), jax.ShapeDtypeStruct((b, t, nk), BF16),
                   jax.ShapeDtypeStruct((b, nk, t), BF16)],
        compiler_params=_cparams(("arbitrary", "arbitrary")),
        name="at_qkv",
    )(x, mod, ng, w_qkv, qk_g, cos, sin)


def _attn_kernel(q_ref, k_ref, vt_ref, o_ref):
    group = q_ref.shape[2] // AT_HEAD
    k = k_ref[0]
    vt = jnp.concatenate([vt_ref[0], jnp.ones((BF16_ROWS, k.shape[0]), BF16)], axis=0)

    kb = min(ATT_K_BLOCK, k.shape[0])
    n_kb = k.shape[0] // kb
    blocks = [(g, b) for g in range(group) for b in range(n_kb)]

    def scores(g, b):
        return _dot_nt(k[b * kb:(b + 1) * kb], q_ref[0, :, g * AT_HEAD:(g + 1) * AT_HEAD])

    pending = [scores(*blk) for blk in blocks[:ATT_AHEAD]]
    m = acc = None
    for i, (g, b) in enumerate(blocks):
        if i + ATT_AHEAD < len(blocks):
            pending.append(scores(*blocks[i + ATT_AHEAD]))
        st = pending.pop(0)
        m_blk = jnp.max(st, axis=0, keepdims=True)
        if b == 0:
            m = m_blk
        else:
            m_new = jnp.maximum(m, m_blk)
            acc = acc * jnp.exp2(m - m_new)
            m = m_new
        part = _dot(vt[:, b * kb:(b + 1) * kb], jnp.exp2(_bf(st - m)))
        acc = part if b == 0 else acc + part
        if b == n_kb - 1:
            o_ref[0, :, g * AT_HEAD:(g + 1) * AT_HEAD] = _bf((acc[:AT_HEAD] / acc[AT_HEAD:AT_HEAD + 1]).T)


def _attention(q, k, vt):
    b, t, d = q.shape
    group = d // AT_HEAD // AT_KVH
    gw = group * AT_HEAD
    tq = min(ATT_Q_TILE, t)
    qblk = pl.BlockSpec((1, tq, gw), lambda bi, h, i: (bi, i, h))
    return pl.pallas_call(
        _attn_kernel,
        grid=(b, AT_KVH, t // tq),
        in_specs=[qblk,
                  pl.BlockSpec((1, t, AT_HEAD), lambda bi, h, i: (bi, 0, h)),
                  pl.BlockSpec((1, AT_HEAD, t), lambda bi, h, i: (bi, h, 0))],
        out_specs=qblk,
        out_shape=jax.ShapeDtypeStruct((b, t, d), BF16),
        compiler_params=_cparams(("arbitrary", "arbitrary", "arbitrary")),
        name="attention",
    )(q, k, vt)


def _at_out_kernel(a_ref, x_ref, mod_ref, ng_ref, wo_ref, o_ref):
    out = _dot(a_ref[0], wo_ref[...])
    gate = mod_ref[0][2:3]
    o_ref[0] = x_ref[0] + gate * _rms(out, ng_ref[...])


def _at_out(a, x, mod, ng, w_o):
    b, t, d = x.shape
    tq = min(ROW_TILE, t)
    row = pl.BlockSpec((1, tq, d), lambda bi, i: (bi, i, 0))
    return pl.pallas_call(
        _at_out_kernel,
        grid=(b, t // tq),
        in_specs=[row, row,
                  pl.BlockSpec((1, N_MOD, d), lambda bi, i: (bi, 0, 0)),
                  pl.BlockSpec((1, d), lambda bi, i: (0, 0)),
                  pl.BlockSpec((d, d), lambda bi, i: (0, 0))],
        out_specs=row,
        out_shape=jax.ShapeDtypeStruct((b, t, d), F32),
        compiler_params=_cparams(("arbitrary", "arbitrary")),
        name="at_out",
    )(a, x, mod, ng, w_o)


def _ffn_kernel(x_ref, mod_ref, ng_ref, wu_ref, wg_ref, cw_ref, cb_ref, wd_ref, o_ref, h_ref):
    f = pl.program_id(1)
    mod = mod_ref[0]

    @pl.when(f == 0)
    def _():
        ng = ng_ref[...]
        h_ref[...] = _bf(_norm_mod(x_ref[0], ng[0:1], mod[3:4], mod[4:5]))
        o_ref[...] = jnp.zeros_like(o_ref)

    t = h_ref.shape[0]
    rows = min(FFN_ROWS, t)
    n_blk = t // rows
    cw = cw_ref[...]
    cb = cb_ref[...]
    zero = jnp.zeros((1, cw.shape[1]), F32)

    def up(b):
        hb = h_ref[b * rows:(b + 1) * rows]
        return _dot(hb, wu_ref[0]), _dot(hb, wg_ref[0])

    ug = [up(0)] + [None] * (n_blk - 1)
    for b in range(n_blk):
        if b + 1 < n_blk:
            ug[b + 1] = up(b + 1)
        u, gv = ug[b]
        before = ug[b - 1][0][rows - 1:rows] if b > 0 else zero
        after = ug[b + 1][0][0:1] if b + 1 < n_blk else zero
        u = cw[0:1] * _shift_rows_down(u, before) + cw[1:2] * u + cw[2:3] * _shift_rows_up(u, after) + cb
        act = u * _sigmoid(u) * gv
        o_ref[0, b * rows:(b + 1) * rows] += _dot(_bf(act), wd_ref[...])

    @pl.when(f == pl.num_programs(1) - 1)
    def _():
        ng = ng_ref[...]
        o_ref[0] = x_ref[0] + mod[5:6] * _rms(o_ref[0], ng[1:2])


def _ffn(x, mod, ng, w_up, conv_w, conv_b, w_down):
    b, t, d = x.shape
    f_dim = w_down.shape[0]
    ft = min(FFN_TILE, f_dim)
    nf = f_dim // ft
    seq = pl.BlockSpec((1, t, d), lambda bi, f: (bi, 0, 0))
    return pl.pallas_call(
        _ffn_kernel,
        grid=(b, nf),
        in_specs=[
            seq,
            pl.BlockSpec((1, N_MOD, d), lambda bi, f: (bi, 0, 0)),
            pl.BlockSpec((2, d), lambda bi, f: (0, 0)),
            pl.BlockSpec((1, d, ft), lambda bi, f: (0, 0, f)),
            pl.BlockSpec((1, d, ft), lambda bi, f: (1, 0, f)),
            pl.BlockSpec((3, ft), lambda bi, f: (0, f)),
            pl.BlockSpec((1, ft), lambda bi, f: (0, f)),
            pl.BlockSpec((ft, d), lambda bi, f: (f, 0)),
        ],
        out_specs=seq,
        out_shape=jax.ShapeDtypeStruct((b, t, d), F32),
        scratch_shapes=[pltpu.VMEM((t, d), BF16)],
        compiler_params=_cparams(("arbitrary", "arbitrary")),
        name="ffn",
    )(x, mod, ng, w_up, w_up, conv_w, conv_b.reshape(1, f_dim), w_down)


def _rope_tables(t):
    pos = jnp.arange(t)
    row = (pos // GRID_W).astype(F32)
    col = (pos % GRID_W).astype(F32)
    n_pair = AT_HEAD // 4
    inv = ROPE_THETA ** (-jnp.arange(n_pair, dtype=F32) / n_pair)
    ang = jnp.concatenate([row[:, None] * inv, col[:, None] * inv], axis=-1)
    cos = jnp.repeat(jnp.cos(ang), 2, axis=-1)
    sin = jnp.repeat(jnp.sin(ang), 2, axis=-1)
    sign = jnp.where(jnp.arange(AT_HEAD) % 2 == 0, -1.0, 1.0)
    return cos, sin * sign


def _trunk(x, mod_all, cos, sin, norm_g, rw, at, ffn):
    v_first = None
    for l in range(mod_all.shape[0]):
        mod = mod_all[l]
        ng = norm_g[l]
        j = l // 2
        if l % 2 == 0:
            proj, vres, ka, lnx, w_o = rw[j]
            if vres is not None:
                vres = (v_first,) + vres
            r, k, v, kk, g, bonus, lw0, lw1, as0, as1 = _rw_proj(x, mod, ng[0:1], *proj, vres)
            if j == 0:
                v_first = v
            y_f = _wkv(r, k, v, kk, lw0, as0, ka, reverse=False)
            y_b = _wkv(r, k, v, kk, lw1, as1, ka, reverse=True)
            x = _rw_out(y_f, y_b, bonus, g, x, mod, ng[1:2], lnx, w_o)
        else:
            w_qkv, qk_g, w_o = at[j]
            q, k, vt = _at_qkv(x, mod, ng[0:1], w_qkv, qk_g, cos, sin)
            x = _at_out(_attention(q, k, vt), x, mod, ng[1:2], w_o)
        x = _ffn(x, mod, ng[2:4], *ffn[l])
    return x


def kernel(x_prompt, x_sample, c_prompt, c_sample, ada_w, ada_b, norm_g, rw_mu, rw_rkv, rw_w0, rw_wA, rw_wB, rw_a0,
           rw_aA, rw_aB, rw_gA, rw_gB, rw_kk, rw_ka, rw_rk, rw_lnx, rw_o, rw_v0, rw_vA, rw_vB, at_qkv, at_qk_g,
           at_o, ffn_up, ffn_conv_w, ffn_conv_b, ffn_down):
    n_prompt = x_prompt.shape[0]
    d = x_prompt.shape[2]
    depth = ada_w.shape[0]
    f_dim = ffn_down.shape[1]
    row = lambda z: z.reshape(1, d)

    rw = []
    for j in range(rw_rkv.shape[0]):
        proj = (rw_mu[j], _bf(rw_rkv[j]), rw_w0[j], _bf(rw_wA[j]), _bf(rw_wB[j]), rw_a0[j], _bf(rw_aA[j]),
                _bf(rw_aB[j]), _bf(rw_gA[j]), _bf(rw_gB[j]), row(rw_kk[j]), row(rw_ka[j]), row(rw_rk[j]))
        vres = None if j == 0 else (row(rw_v0[j - 1]), _bf(rw_vA[j - 1]), _bf(rw_vB[j - 1]))
        rw.append((proj, vres, row(rw_ka[j]), rw_lnx[j], _bf(rw_o[j])))
    at = [(_bf(at_qkv[j]), at_qk_g[j], _bf(at_o[j])) for j in range(at_qkv.shape[0])]
    ffn = [(_bf(ffn_up[l]).reshape(d, 2, f_dim).transpose(1, 0, 2), ffn_conv_w[l], ffn_conv_b[l], _bf(ffn_down[l]))
           for l in range(depth)]

    c = jnp.concatenate([c_prompt, c_sample], axis=0)
    mod_all = _ada_mod(c, ada_w, ada_b).reshape(depth, c.shape[0], N_MOD, d)
    outs = []
    for x, mod in ((x_prompt, mod_all[:, :n_prompt]), (x_sample, mod_all[:, n_prompt:])):
        cos, sin = _rope_tables(x.shape[1])
        outs.append(_trunk(x, mod, cos, sin, norm_g, rw, at, ffn))
    return tuple(outs)
```

```python
import functools

import jax
import jax.numpy as jnp
from jax import lax
from jax.experimental import pallas as pl
from jax.experimental.pallas import tpu as pltpu

F32 = jnp.float32
BF16 = jnp.bfloat16

RW_HEAD = 64
AT_HEAD = 128
AT_KVH = 2
GRID_W = 64
ROPE_THETA = 10000.0
NORM_EPS = 1e-6
GN_EPS = 64e-5
N_MOD = 6

LOG2_E = 1.4426950408889634
DECAY_CAP = 0.6065306597126334
LANES = 128
BF16_ROWS = 16
HEAD_SHIFT = RW_HEAD.bit_length() - 1
WKV_CHUNK = 64
WKV_BLOCK = 256
WKV_SHARE = 0
WKV_GROUP = 4
WKV_SKEW = 0
ROW_TILE = 256
RW_SUB_ROWS = 256
ATT_Q_TILE = 256
ATT_K_BLOCK = 256
ATT_AHEAD = 8
FFN_TILE = 256
FFN_ROWS = 1024
VMEM_LIMIT = 56 * 1024 * 1024


def _dot(a, b):
    return jnp.dot(a, b, preferred_element_type=F32)


def _dot_nt(a, b):
    return lax.dot_general(a, b, (((1,), (1,)), ((), ())), preferred_element_type=F32)


def _dot_tn(a, b):
    return lax.dot_general(a, b, (((0,), (0,)), ((), ())), preferred_element_type=F32)


def _bf(x):
    return x.astype(BF16)


def _sigmoid(x):
    return 1.0 / (1.0 + jnp.exp(-x))


def _norm_mod(x, gain, shift, scale):
    ms = jnp.mean(x * x, axis=-1, keepdims=True)
    return (x * lax.rsqrt(ms + NORM_EPS)) * (gain * (1.0 + scale)) + shift


def _rms(x, gain):
    ms = jnp.mean(x * x, axis=-1, keepdims=True)
    return x * lax.rsqrt(ms + NORM_EPS) * gain


def _head_ones():
    r = lax.broadcasted_iota(jnp.int32, (LANES, LANES), 0) >> HEAD_SHIFT
    c = lax.broadcasted_iota(jnp.int32, (LANES, LANES), 1) >> HEAD_SHIFT
    return jnp.where(r == c, 1.0, 0.0).astype(BF16)


def _head_sum(x, ones_bd):
    hi = _bf(x)
    lo = _bf(x - hi.astype(F32))
    out = []
    for p in range(x.shape[1] // LANES):
        sl = slice(p * LANES, (p + 1) * LANES)
        out.append(_dot(hi[:, sl], ones_bd) + _dot(lo[:, sl], ones_bd))
    return jnp.concatenate(out, axis=1)


def _shift_rows_down(x, first_row):
    y = pltpu.roll(x, 1, axis=0)
    row = lax.broadcasted_iota(jnp.int32, (8, x.shape[1]), 0)
    head = jnp.where(row == 0, first_row, y[:8])
    return jnp.concatenate([head, y[8:]], axis=0)


def _shift_rows_up(x, last_row):
    n = x.shape[0]
    y = pltpu.roll(x, n - 1, axis=0)
    row = lax.broadcasted_iota(jnp.int32, (8, x.shape[1]), 0)
    tail = jnp.where(row == 7, last_row, y[n - 8:])
    return jnp.concatenate([y[:n - 8], tail], axis=0)


def _cparams(sem):
    return pltpu.CompilerParams(dimension_semantics=sem, vmem_limit_bytes=VMEM_LIMIT)


def _mod_kernel(c_ref, w_ref, b_ref, o_ref):
    c = c_ref[...]
    sc = c * _sigmoid(c)
    o_ref[0] = _dot(_bf(sc), _bf(w_ref[0])) + b_ref[0]


def _ada_mod(c, ada_w, ada_b):
    depth, d, n = ada_w.shape
    b = c.shape[0]
    tn = n // 4
    return pl.pallas_call(
        _mod_kernel,
        grid=(depth, n // tn),
        in_specs=[
            pl.BlockSpec((b, d), lambda l, j: (0, 0)),
            pl.BlockSpec((1, d, tn), lambda l, j: (l, 0, j)),
            pl.BlockSpec((1, 1, tn), lambda l, j: (l, 0, j)),
        ],
        out_specs=pl.BlockSpec((1, b, tn), lambda l, j: (l, 0, j)),
        out_shape=jax.ShapeDtypeStruct((depth, b, n), F32),
        compiler_params=_cparams(("arbitrary", "arbitrary")),
        name="ada_mod",
    )(c, ada_w, ada_b.reshape(depth, 1, n))


def _rw_proj_kernel(has_vres, x_ref, xp_ref, xn_ref, mod_ref, ng_ref, mu_ref, wrkv_ref, w0_ref, wA_ref, wB_ref,
                    a0_ref, aA_ref, aB_ref, gA_ref, gB_ref, kk_ref, ka_ref, rk_ref, *rest):
    if has_vres:
        vf_ref, v0_ref, vA_ref, vB_ref = rest[:4]
        rest = rest[4:]
    r_o, k_o, v_o, kkn_o, g_o, bonus_o, lw0_o, lw1_o, as0_o, as1_o = rest
    i = pl.program_id(1)
    nt = pl.num_programs(1)
    mod = mod_ref[0]
    shift, scale = mod[0:1], mod[1:2]
    gain = ng_ref[...]
    h = _norm_mod(x_ref[0], gain, shift, scale)
    h_before = _norm_mod(xp_ref[0], gain, shift, scale)[7:8]
    h_after = _norm_mod(xn_ref[0], gain, shift, scale)[0:1]
    h_before = jnp.where(i == 0, 0.0, h_before)
    h_after = jnp.where(i == nt - 1, 0.0, h_after)
    xx = 0.5 * (_shift_rows_down(h, h_before) + _shift_rows_up(h, h_after)) - h
    mu = mu_ref[...]
    ones_bd = _head_ones()
    ka = ka_ref[...]
    n_lora = w0_ref.shape[1] // 2

    rows = h.shape[0]
    sub = min(RW_SUB_ROWS, rows)
    for s0 in range(0, rows, sub):
        rs = slice(s0, s0 + sub)
        hs, xs = h[rs], xx[rs]
        xr, xw, xk, xv, xa, xg = (_bf(hs + xs * mu[n:n + 1]) for n in range(6))
        r = _dot(xr, wrkv_ref[0])
        k = _dot(xk, wrkv_ref[1])
        v = _dot(xv, wrkv_ref[2])
        if has_vres:
            mix = _sigmoid(v0_ref[...] + _dot(_bf(_dot(xv, vA_ref[...])), vB_ref[...]))
            v = v + (vf_ref[0, rs] - v) * mix
        r_o[0, rs] = r
        k_o[0, rs] = k
        v_o[0, rs] = v
        g_o[0, rs] = _dot(_bf(_sigmoid(_dot(xg, gA_ref[...]))), gB_ref[...])
        kk = k * kk_ref[...]
        kkn_o[0, rs] = kk * lax.rsqrt(jnp.maximum(_head_sum(kk * kk, ones_bd), 1e-24))
        wl = w0_ref[...] + _dot(_bf(jnp.tanh(_dot(xw, wA_ref[...]))), wB_ref[...])
        lw = -DECAY_CAP * _sigmoid(wl)
        a_sig = _sigmoid(a0_ref[...] + _dot(_bf(_dot(xa, aA_ref[...])), aB_ref[...]))
        lw0_o[0, rs], lw1_o[0, rs] = lw[:, :n_lora], lw[:, n_lora:]
        as0, as1 = a_sig[:, :n_lora], a_sig[:, n_lora:]
        as0_o[0, rs], as1_o[0, rs] = as0, as1
        kd_sum = k * (2.0 + (as0 + as1 - 2.0) * ka)
        bonus_o[0, rs] = _head_sum(r * kd_sum * rk_ref[...], ones_bd) * v


def _rw_proj(x, mod, ng, mu, w_rkv, w0, wA, wB, a0, aA, aB, gA, gB, k_k, k_a, r_k, vres):
    b, t, d = x.shape
    tq = min(ROW_TILE, t)
    nt = t // tq
    has_vres = vres is not None
    row = pl.BlockSpec((1, tq, d), lambda bi, i: (bi, i, 0))

    def full(a):
        nd = a.ndim
        return pl.BlockSpec(a.shape, lambda bi, i: (0,) * nd)

    halo = tq // 8
    n8 = t // 8
    in_specs = [
        row,
        pl.BlockSpec((1, 8, d), lambda bi, i: (bi, jnp.maximum(i * halo - 1, 0), 0)),
        pl.BlockSpec((1, 8, d), lambda bi, i: (bi, jnp.minimum((i + 1) * halo, n8 - 1), 0)),
        pl.BlockSpec((1, N_MOD, d), lambda bi, i: (bi, 0, 0)),
    ]
    params = [ng, mu, w_rkv, w0, wA, wB, a0, aA, aB, gA, gB, k_k, k_a, r_k]
    args = [x, x, x, mod] + params
    in_specs += [full(a) for a in params]
    if has_vres:
        v_first, v0, vA, vB = vres
        args += [v_first, v0, vA, vB]
        in_specs += [row, full(v0), full(vA), full(vB)]
    n_out = 10
    return pl.pallas_call(
        functools.partial(_rw_proj_kernel, has_vres),
        grid=(b, nt),
        in_specs=in_specs,
        out_specs=[row] * n_out,
        out_shape=[jax.ShapeDtypeStruct((b, t, d), F32)] * n_out,
        compiler_params=_cparams(("arbitrary", "arbitrary")),
        name="rw_proj",
    )(*args)


def _wkv_kernel(reverse, add_to, r_ref, k_ref, v_ref, kk_ref, lw_ref, as_ref, ka_ref, tri_ref, *rest):
    ya_ref = rest[0] if add_to else None
    y_ref, s_ref = rest[-2:]
    L = WKV_CHUNK
    rows, d = lw_ref.shape[1], lw_ref.shape[2]
    n_chunk = rows // L
    n_tile = d // LANES

    @pl.when(pl.program_id(1) == 0)
    def _():
        s_ref[...] = jnp.zeros_like(s_ref)

    lw = lw_ref[0]
    hi = _bf(lw)
    rem = lw - hi.astype(F32)
    mid = _bf(rem)
    lo = _bf(rem - mid.astype(F32))
    tri = tri_ref[...]
    cs = _dot(tri, hi) + _dot(tri, mid) + _dot(tri, lo)

    w_in = jnp.exp(cs)
    w_inv = jnp.exp(-cs)
    w_ex = w_in * jnp.exp(-lw)
    ends = [c * L if reverse else (c + 1) * L - 1 for c in range(n_chunk)]
    dec = [jnp.exp(cs[e:e + 1]) for e in ends]
    kk = kk_ref[0]
    a_sig = as_ref[0]
    v = v_ref[0]
    a_t = -(kk * w_ex)
    r_t = r_ref[0] * w_in
    b_t = kk * a_sig * w_inv
    k_t = k_ref[0] * (1.0 + (a_sig - 1.0) * ka_ref[...]) * w_inv

    lane = lax.broadcasted_iota(jnp.int32, (L, LANES), 1)
    row = lax.broadcasted_iota(jnp.int32, (L, LANES), 0)
    col = lane & (L - 1)
    head0 = lane < RW_HEAD
    if reverse:
        strict, incl = row < col, row <= col
    else:
        strict, incl = row > col, row >= col
    eye2 = jnp.where(row == col, 1.0, 0.0)
    r2 = lax.broadcasted_iota(jnp.int32, (LANES, LANES), 0) >> HEAD_SHIFT
    c2 = lax.broadcasted_iota(jnp.int32, (LANES, LANES), 1) >> HEAD_SHIFT
    same_head = r2 == c2

    def bdiag(z):
        return _bf(jnp.concatenate([jnp.where(head0, z, 0.0), jnp.where(head0, 0.0, z)], axis=0))

    chunks = list(range(n_chunk - 1, -1, -1) if reverse else range(n_chunk))
    group_chunks = [chunks[g:g + WKV_GROUP] for g in range(0, n_chunk, WKV_GROUP)]
    done = {}

    def chains(g):
        pairs = [(c, p) for c in group_chunks[g] for p in range(n_tile)]
        idx = range(len(pairs))

        def cut(z):
            return [z[c * L:(c + 1) * L, p * LANES:(p + 1) * LANES] for c, p in pairs]

        def dots(lhs, w):
            m = lhs[0].shape[0]
            out = [None] * len(idx)
            n_share = min(WKV_SHARE, len(idx)) // 2 * 2
            for i in range(0, n_share, 2):
                o = _dot(jnp.concatenate([lhs[i], lhs[i + 1]], axis=0), jnp.concatenate([w[i], w[i + 1]], axis=1))
                out[i], out[i + 1] = o[:m, :LANES], o[m:, LANES:]
            for i in range(n_share, len(idx)):
                out[i] = _dot(lhs[i], w[i])
            return out

        at, rt, bt, kt, vv = (cut(z) for z in (a_t, r_t, b_t, k_t, v))
        dec_t = [dec[c][:, p * LANES:(p + 1) * LANES] for c, p in pairs]
        bh = [bt[i] * dec_t[i] for i in idx]
        kh = [kt[i] * dec_t[i] for i in idx]
        vbd = [bdiag(vv[i]) for i in idx]
        gram = [_dot(_bf(jnp.concatenate([at[i], rt[i]], axis=0)),
                     jnp.concatenate([bdiag(bt[i]).T, bdiag(kt[i]).T], axis=1)) for i in idx]
        yield
        a_ab = [jnp.where(strict, gram[i][:L, :LANES], 0.0) for i in idx]
        a_ak = [jnp.where(strict, gram[i][:L, LANES:], 0.0) for i in idx]
        a_rb = [_bf(jnp.where(incl, gram[i][L:, :LANES], 0.0)) for i in idx]
        a_rk = [jnp.where(incl, gram[i][L:, LANES:], 0.0) for i in idx]
        av = dots([_bf(jnp.concatenate([a_ak[i], a_rk[i]], axis=0)) for i in idx], vbd)
        akv = [av[i][:L] for i in idx]
        y0 = [av[i][L:] for i in idx]
        yield
        x_inv = [eye2 + jnp.where((row >> 1) == (col >> 1), a_ab[i], 0.0) for i in idx]
        lev = 1
        while (1 << lev) < L:
            sel = ((row >> (lev + 1)) == (col >> (lev + 1))) & ((row >> lev) != (col >> lev))
            step = dots([_bf(x_inv[i]) for i in idx], [bdiag(jnp.where(sel, a_ab[i], 0.0)) for i in idx])
            yield
            grow = dots([_bf(step[i]) for i in idx], [bdiag(x_inv[i]) for i in idx])
            yield
            x_inv = [x_inv[i] + grow[i] for i in idx]
            lev += 1
        sol = [_dot(_bf(x_inv[i]), jnp.concatenate([bdiag(at[i]), bdiag(akv[i])], axis=1)) for i in idx]
        yield
        at2 = [sol[i][:, :LANES] for i in idx]
        u0 = [sol[i][:, LANES:] for i in idx]
        mix = [_dot(a_rb[i], jnp.concatenate([bdiag(at2[i]), bdiag(u0[i])], axis=1)) for i in idx]
        yield
        r2_ = [_bf(rt[i] + mix[i][:, :LANES]) for i in idx]
        y0 = [y0[i] + mix[i][:, LANES:] for i in idx]
        p_bd = [_bf(jnp.where(same_head, _dot_tn(_bf(at2[i]), _bf(bh[i])), 0.0)) for i in idx]
        yield
        q_bd = [jnp.where(same_head,
                          _dot_tn(_bf(jnp.concatenate([u0[i], vv[i]], axis=0)),
                                  _bf(jnp.concatenate([bh[i], kh[i]], axis=0))), 0.0) for i in idx]
        done[g] = (pairs, r2_, y0, p_bd, q_bd, dec_t)

    def state_steps():
        state = [s_ref[p] for p in range(n_tile)]
        for g in range(len(group_chunks)):
            while g not in done:
                yield
            pairs, r2_, y0, p_bd, q_bd, dec_t = done[g]
            for i, (c, p) in enumerate(pairs):
                s0 = state[p]
                s0b = _bf(s0)
                blk = (0, slice(c * L, (c + 1) * L), slice(p * LANES, (p + 1) * LANES))
                y = _dot(r2_[i], s0b.T) + y0[i]
                y_ref[blk] = y + ya_ref[blk] if add_to else y
                state[p] = s0 * dec_t[i] + _dot(s0b, p_bd[i]) + q_bd[i]
                if p == n_tile - 1:
                    yield
        for p in range(n_tile):
            s_ref[p] = state[p]

    active = [state_steps()]
    started = 0
    tick = 0
    while active:
        if started < len(group_chunks) and tick >= started * WKV_SKEW:
            active.insert(len(active) - 1, chains(started))
            started += 1
        for gen in list(active):
            try:
                next(gen)
            except StopIteration:
                active.remove(gen)
        tick += 1


def _wkv_tri(rows, reverse):
    t = jnp.arange(rows)
    same = (t[:, None] // WKV_CHUNK) == (t[None, :] // WKV_CHUNK)
    tri = (t[:, None] <= t[None, :]) if reverse else (t[:, None] >= t[None, :])
    return (same & tri).astype(BF16)


def _wkv(r, k, v, kk, lw, a_sig, k_a, reverse, add_to=None):
    b, t, d = r.shape
    rows = min(WKV_BLOCK, t)
    nb = t // rows
    if reverse:
        blk = pl.BlockSpec((1, rows, d), lambda bi, i: (bi, nb - 1 - i, 0))
    else:
        blk = pl.BlockSpec((1, rows, d), lambda bi, i: (bi, i, 0))
    args = [r, k, v, kk, lw, a_sig, k_a, _wkv_tri(rows, reverse)]
    in_specs = [blk] * 6 + [
        pl.BlockSpec((1, d), lambda bi, i: (0, 0)),
        pl.BlockSpec((rows, rows), lambda bi, i: (0, 0)),
    ]
    aliases = {}
    if add_to is not None:
        aliases = {len(args): 0}
        args.append(add_to)
        in_specs.append(blk)
    return pl.pallas_call(
        functools.partial(_wkv_kernel, reverse, add_to is not None),
        grid=(b, nb),
        in_specs=in_specs,
        out_specs=blk,
        out_shape=jax.ShapeDtypeStruct((b, t, d), F32),
        scratch_shapes=[pltpu.VMEM((d // LANES, LANES, LANES), F32)],
        input_output_aliases=aliases,
        compiler_params=_cparams(("arbitrary", "arbitrary")),
        name="wkv_bwd" if reverse else "wkv_fwd",
    )(*args)


def _rw_out_kernel(y_ref, bonus_ref, g_ref, x_ref, mod_ref, ng_ref, lnx_ref, wo_ref, o_ref):
    ones_bd = _head_ones()
    y = y_ref[0]
    inv_n = 1.0 / RW_HEAD
    mean = _head_sum(y, ones_bd) * inv_n
    yc = y - mean
    var = _head_sum(yc * yc, ones_bd) * inv_n
    lnx = lnx_ref[...]
    yn = yc * lax.rsqrt(var + GN_EPS) * lnx[0:1] + lnx[1:2]
    o = (yn + bonus_ref[0]) * g_ref[0]
    out = _dot(_bf(o), wo_ref[...])
    gate = mod_ref[0][2:3]
    o_ref[0] = x_ref[0] + gate * _rms(out, ng_ref[...])


def _rw_out(y, bonus, g, x, mod, ng, lnx, w_o):
    b, t, d = x.shape
    tq = min(ROW_TILE, t)
    row = pl.BlockSpec((1, tq, d), lambda bi, i: (bi, i, 0))
    return pl.pallas_call(
        _rw_out_kernel,
        grid=(b, t // tq),
        in_specs=[row] * 4 + [
            pl.BlockSpec((1, N_MOD, d), lambda bi, i: (bi, 0, 0)),
            pl.BlockSpec((1, d), lambda bi, i: (0, 0)),
            pl.BlockSpec((2, d), lambda bi, i: (0, 0)),
            pl.BlockSpec((d, d), lambda bi, i: (0, 0)),
        ],
        out_specs=row,
        out_shape=jax.ShapeDtypeStruct((b, t, d), F32),
        compiler_params=_cparams(("arbitrary", "arbitrary")),
        name="rw_out",
    )(y, bonus, g, x, mod, ng, lnx, w_o)


def _rope(x, cos, sin_signed):
    lane = lax.broadcasted_iota(jnp.int32, x.shape, 1)
    partner = jnp.where((lane & 1) == 0, pltpu.roll(x, LANES - 1, axis=1), pltpu.roll(x, 1, axis=1))
    return x * cos + partner * sin_signed


def _at_qkv_kernel(x_ref, mod_ref, ng_ref, w_ref, qkg_ref, cos_ref, sin_ref, q_o, k_o, vt_o):
    mod = mod_ref[0]
    h = _norm_mod(x_ref[0], ng_ref[...], mod[0:1], mod[1:2])
    qkv = _dot(_bf(h), w_ref[...])
    nq = q_o.shape[2]
    nk = k_o.shape[2]
    cos, sin = cos_ref[...], sin_ref[...]
    qkg = qkg_ref[...]
    scale = AT_HEAD ** -0.5 * LOG2_E

    def head(z, gain, mul):
        ms = jnp.mean(z * z, axis=-1, keepdims=True)
        return _rope(z * lax.rsqrt(ms + NORM_EPS) * gain, cos, sin) * mul

    for n in range(nq // AT_HEAD):
        sl = slice(n * AT_HEAD, (n + 1) * AT_HEAD)
        q_o[0, :, sl] = _bf(head(qkv[:, sl], qkg[0:1], scale))
    for n in range(nk // AT_HEAD):
        sl = slice(n * AT_HEAD, (n + 1) * AT_HEAD)
        k_o[0, :, sl] = _bf(head(qkv[:, nq + n * AT_HEAD:nq + (n + 1) * AT_HEAD], qkg[1:2], 1.0))
    vt_o[0] = _bf(qkv[:, nq + nk:].T)


def _at_qkv(x, mod, ng, w_qkv, qk_g, cos, sin):
    b, t, d = x.shape
    tq = min(ROW_TILE, t)
    nk = AT_KVH * AT_HEAD
    row = lambda n: pl.BlockSpec((1, tq, n), lambda bi, i: (bi, i, 0))
    return pl.pallas_call(
        _at_qkv_kernel,
        grid=(b, t // tq),
        in_specs=[
            row(d),
            pl.BlockSpec((1, N_MOD, d), lambda bi, i: (bi, 0, 0)),
            pl.BlockSpec((1, d), lambda bi, i: (0, 0)),
            pl.BlockSpec(w_qkv.shape, lambda bi, i: (0, 0)),
            pl.BlockSpec((2, AT_HEAD), lambda bi, i: (0, 0)),
            pl.BlockSpec((tq, AT_HEAD), lambda bi, i: (i, 0)),
            pl.BlockSpec((tq, AT_HEAD), lambda bi, i: (i, 0)),
        ],
        out_specs=[row(d), row(nk), pl.BlockSpec((1, nk, tq), lambda bi, i: (bi, 0, i))],
        out_shape=[jax.ShapeDtypeStruct((b, t, d), BF16), jax.ShapeDtypeStruct((b, t, nk), BF16),
                   jax.ShapeDtypeStruct((b, nk, t), BF16)],
        compiler_params=_cparams(("arbitrary", "arbitrary")),
        name="at_qkv",
    )(x, mod, ng, w_qkv, qk_g, cos, sin)


def _attn_kernel(q_ref, k_ref, vt_ref, o_ref):
    group = q_ref.shape[2] // AT_HEAD
    k = k_ref[0]
    vt = jnp.concatenate([vt_ref[0], jnp.ones((BF16_ROWS, k.shape[0]), BF16)], axis=0)

    kb = min(ATT_K_BLOCK, k.shape[0])
    n_kb = k.shape[0] // kb
    blocks = [(g, b) for g in range(group) for b in range(n_kb)]

    def scores(g, b):
        return _dot_nt(k[b * kb:(b + 1) * kb], q_ref[0, :, g * AT_HEAD:(g + 1) * AT_HEAD])

    pending = [scores(*blk) for blk in blocks[:ATT_AHEAD]]
    m = acc = None
    for i, (g, b) in enumerate(blocks):
        if i + ATT_AHEAD < len(blocks):
            pending.append(scores(*blocks[i + ATT_AHEAD]))
        st = pending.pop(0)
        m_blk = jnp.max(st, axis=0, keepdims=True)
        if b == 0:
            m = m_blk
        else:
            m_new = jnp.maximum(m, m_blk)
            acc = acc * jnp.exp2(m - m_new)
            m = m_new
        part = _dot(vt[:, b * kb:(b + 1) * kb], jnp.exp2(_bf(st - m)))
        acc = part if b == 0 else acc + part
        if b == n_kb - 1:
            o_ref[0, :, g * AT_HEAD:(g + 1) * AT_HEAD] = _bf((acc[:AT_HEAD] / acc[AT_HEAD:AT_HEAD + 1]).T)


def _attention(q, k, vt):
    b, t, d = q.shape
    group = d // AT_HEAD // AT_KVH
    gw = group * AT_HEAD
    tq = min(ATT_Q_TILE, t)
    qblk = pl.BlockSpec((1, tq, gw), lambda bi, h, i: (bi, i, h))
    return pl.pallas_call(
        _attn_kernel,
        grid=(b, AT_KVH, t // tq),
        in_specs=[qblk,
                  pl.BlockSpec((1, t, AT_HEAD), lambda bi, h, i: (bi, 0, h)),
                  pl.BlockSpec((1, AT_HEAD, t), lambda bi, h, i: (bi, h, 0))],
        out_specs=qblk,
        out_shape=jax.ShapeDtypeStruct((b, t, d), BF16),
        compiler_params=_cparams(("arbitrary", "arbitrary", "arbitrary")),
        name="attention",
    )(q, k, vt)


def _at_out_kernel(a_ref, x_ref, mod_ref, ng_ref, wo_ref, o_ref):
    out = _dot(a_ref[0], wo_ref[...])
    gate = mod_ref[0][2:3]
    o_ref[0] = x_ref[0] + gate * _rms(out, ng_ref[...])


def _at_out(a, x, mod, ng, w_o):
    b, t, d = x.shape
    tq = min(ROW_TILE, t)
    row = pl.BlockSpec((1, tq, d), lambda bi, i: (bi, i, 0))
    return pl.pallas_call(
        _at_out_kernel,
        grid=(b, t // tq),
        in_specs=[row, row,
                  pl.BlockSpec((1, N_MOD, d), lambda bi, i: (bi, 0, 0)),
                  pl.BlockSpec((1, d), lambda bi, i: (0, 0)),
                  pl.BlockSpec((d, d), lambda bi, i: (0, 0))],
        out_specs=row,
        out_shape=jax.ShapeDtypeStruct((b, t, d), F32),
        compiler_params=_cparams(("arbitrary", "arbitrary")),
        name="at_out",
    )(a, x, mod, ng, w_o)


def _ffn_kernel(x_ref, mod_ref, ng_ref, wu_ref, wg_ref, cw_ref, cb_ref, wd_ref, o_ref, h_ref):
    f = pl.program_id(1)
    mod = mod_ref[0]

    @pl.when(f == 0)
    def _():
        ng = ng_ref[...]
        h_ref[...] = _bf(_norm_mod(x_ref[0], ng[0:1], mod[3:4], mod[4:5]))
        o_ref[...] = jnp.zeros_like(o_ref)

    t = h_ref.shape[0]
    rows = min(FFN_ROWS, t)
    n_blk = t // rows
    cw = cw_ref[...]
    cb = cb_ref[...]
    zero = jnp.zeros((1, cw.shape[1]), F32)

    def up(b):
        hb = h_ref[b * rows:(b + 1) * rows]
        return _dot(hb, wu_ref[0]), _dot(hb, wg_ref[0])

    ug = [up(0)] + [None] * (n_blk - 1)
    for b in range(n_blk):
        if b + 1 < n_blk:
            ug[b + 1] = up(b + 1)
        u, gv = ug[b]
        before = ug[b - 1][0][rows - 1:rows] if b > 0 else zero
        after = ug[b + 1][0][0:1] if b + 1 < n_blk else zero
        u = cw[0:1] * _shift_rows_down(u, before) + cw[1:2] * u + cw[2:3] * _shift_rows_up(u, after) + cb
        act = u * _sigmoid(u) * gv
        o_ref[0, b * rows:(b + 1) * rows] += _dot(_bf(act), wd_ref[...])

    @pl.when(f == pl.num_programs(1) - 1)
    def _():
        ng = ng_ref[...]
        o_ref[0] = x_ref[0] + mod[5:6] * _rms(o_ref[0], ng[1:2])


def _ffn(x, mod, ng, w_up, conv_w, conv_b, w_down):
    b, t, d = x.shape
    f_dim = w_down.shape[0]
    ft = min(FFN_TILE, f_dim)
    nf = f_dim // ft
    seq = pl.BlockSpec((1, t, d), lambda bi, f: (bi, 0, 0))
    return pl.pallas_call(
        _ffn_kernel,
        grid=(b, nf),
        in_specs=[
            seq,
            pl.BlockSpec((1, N_MOD, d), lambda bi, f: (bi, 0, 0)),
            pl.BlockSpec((2, d), lambda bi, f: (0, 0)),
            pl.BlockSpec((1, d, ft), lambda bi, f: (0, 0, f)),
            pl.BlockSpec((1, d, ft), lambda bi, f: (1, 0, f)),
            pl.BlockSpec((3, ft), lambda bi, f: (0, f)),
            pl.BlockSpec((1, ft), lambda bi, f: (0, f)),
            pl.BlockSpec((ft, d), lambda bi, f: (f, 0)),
        ],
        out_specs=seq,
        out_shape=jax.ShapeDtypeStruct((b, t, d), F32),
        scratch_shapes=[pltpu.VMEM((t, d), BF16)],
        compiler_params=_cparams(("arbitrary", "arbitrary")),
        name="ffn",
    )(x, mod, ng, w_up, w_up, conv_w, conv_b.reshape(1, f_dim), w_down)


def _rope_tables(t):
    pos = jnp.arange(t)
    row = (pos // GRID_W).astype(F32)
    col = (pos % GRID_W).astype(F32)
    n_pair = AT_HEAD // 4
    inv = ROPE_THETA ** (-jnp.arange(n_pair, dtype=F32) / n_pair)
    ang = jnp.concatenate([row[:, None] * inv, col[:, None] * inv], axis=-1)
    cos = jnp.repeat(jnp.cos(ang), 2, axis=-1)
    sin = jnp.repeat(jnp.sin(ang), 2, axis=-1)
    sign = jnp.where(jnp.arange(AT_HEAD) % 2 == 0, -1.0, 1.0)
    return cos, sin * sign


def _trunk(x, mod_all, cos, sin, norm_g, rw, at, ffn):
    v_first = None
    for l in range(mod_all.shape[0]):
        mod = mod_all[l]
        ng = norm_g[l]
        j = l // 2
        if l % 2 == 0:
            proj, vres, ka, lnx, w_o = rw[j]
            if vres is not None:
                vres = (v_first,) + vres
            r, k, v, kk, g, bonus, lw0, lw1, as0, as1 = _rw_proj(x, mod, ng[0:1], *proj, vres)
            if j == 0:
                v_first = v
            y = _wkv(r, k, v, kk, lw0, as0, ka, reverse=False)
            y = _wkv(r, k, v, kk, lw1, as1, ka, reverse=True, add_to=y)
            x = _rw_out(y, bonus, g, x, mod, ng[1:2], lnx, w_o)
        else:
            w_qkv, qk_g, w_o = at[j]
            q, k, vt = _at_qkv(x, mod, ng[0:1], w_qkv, qk_g, cos, sin)
            x = _at_out(_attention(q, k, vt), x, mod, ng[1:2], w_o)
        x = _ffn(x, mod, ng[2:4], *ffn[l])
    return x


def kernel(x_prompt, x_sample, c_prompt, c_sample, ada_w, ada_b, norm_g, rw_mu, rw_rkv, rw_w0, rw_wA, rw_wB, rw_a0,
           rw_aA, rw_aB, rw_gA, rw_gB, rw_kk, rw_ka, rw_rk, rw_lnx, rw_o, rw_v0, rw_vA, rw_vB, at_qkv, at_qk_g,
           at_o, ffn_up, ffn_conv_w, ffn_conv_b, ffn_down):
    n_prompt = x_prompt.shape[0]
    d = x_prompt.shape[2]
    depth = ada_w.shape[0]
    f_dim = ffn_down.shape[1]
    row = lambda z: z.reshape(1, d)

    def both_dirs(bias, w_in, w_out):
        zero = jnp.zeros_like(w_out[0])
        w_bd = jnp.concatenate([jnp.concatenate([w_out[0], zero], axis=1),
                                jnp.concatenate([zero, w_out[1]], axis=1)], axis=0)
        return bias.reshape(1, 2 * d), _bf(jnp.concatenate([w_in[0], w_in[1]], axis=1)), _bf(w_bd)

    rw = []
    for j in range(rw_rkv.shape[0]):
        proj = (rw_mu[j], _bf(rw_rkv[j]), *both_dirs(rw_w0[j], rw_wA[j], rw_wB[j]),
                *both_dirs(rw_a0[j], rw_aA[j], rw_aB[j]), _bf(rw_gA[j]), _bf(rw_gB[j]), row(rw_kk[j]),
                row(rw_ka[j]), row(rw_rk[j]))
        vres = None if j == 0 else (row(rw_v0[j - 1]), _bf(rw_vA[j - 1]), _bf(rw_vB[j - 1]))
        rw.append((proj, vres, row(rw_ka[j]), rw_lnx[j], _bf(rw_o[j])))
    at = [(_bf(at_qkv[j]), at_qk_g[j], _bf(at_o[j])) for j in range(at_qkv.shape[0])]
    ffn = [(_bf(ffn_up[l]).reshape(d, 2, f_dim).transpose(1, 0, 2), ffn_conv_w[l], ffn_conv_b[l], _bf(ffn_down[l]))
           for l in range(depth)]

    c = jnp.concatenate([c_prompt, c_sample], axis=0)
    mod_all = _ada_mod(c, ada_w, ada_b).reshape(depth, c.shape[0], N_MOD, d)
    outs = []
    for x, mod in ((x_prompt, mod_all[:, :n_prompt]), (x_sample, mod_all[:, n_prompt:])):
        cos, sin = _rope_tables(x.shape[1])
        outs.append(_trunk(x, mod, cos, sin, norm_g, rw, at, ffn))
    return tuple(outs)
```

```python
import functools

import jax
import jax.numpy as jnp
from jax import lax
from jax.experimental import pallas as pl
from jax.experimental.pallas import tpu as pltpu

F32 = jnp.float32
BF16 = jnp.bfloat16

RW_HEAD = 64
AT_HEAD = 128
AT_KVH = 2
GRID_W = 64
ROPE_THETA = 10000.0
NORM_EPS = 1e-6
GN_EPS = 64e-5
N_MOD = 6

LOG2_E = 1.4426950408889634
DECAY_CAP = 0.6065306597126334
LANES = 128
BF16_ROWS = 16
HEAD_SHIFT = RW_HEAD.bit_length() - 1
WKV_CHUNK = 64
WKV_BLOCK = 256
WKV_SHARE = 0
WKV_GROUP = 4
WKV_SKEW = 0
ROW_TILE = 256
RW_SUB_ROWS = 256
ATT_Q_TILE = 256
ATT_K_BLOCK = 256
ATT_AHEAD = 8
FFN_TILE = 256
FFN_ROWS = 512
FFN_SUB_ROWS = 256
FFN_AHEAD = 2
VMEM_LIMIT = 56 * 1024 * 1024


def _dot(a, b):
    return jnp.dot(a, b, preferred_element_type=F32)


def _dot_nt(a, b):
    return lax.dot_general(a, b, (((1,), (1,)), ((), ())), preferred_element_type=F32)


def _dot_tn(a, b):
    return lax.dot_general(a, b, (((0,), (0,)), ((), ())), preferred_element_type=F32)


def _bf(x):
    return x.astype(BF16)


def _sigmoid(x):
    return 1.0 / (1.0 + jnp.exp(-x))


def _norm_mod(x, gain, shift, scale):
    ms = jnp.mean(x * x, axis=-1, keepdims=True)
    return (x * lax.rsqrt(ms + NORM_EPS)) * (gain * (1.0 + scale)) + shift


def _rms(x, gain):
    ms = jnp.mean(x * x, axis=-1, keepdims=True)
    return x * lax.rsqrt(ms + NORM_EPS) * gain


def _head_ones():
    r = lax.broadcasted_iota(jnp.int32, (LANES, LANES), 0) >> HEAD_SHIFT
    c = lax.broadcasted_iota(jnp.int32, (LANES, LANES), 1) >> HEAD_SHIFT
    return jnp.where(r == c, 1.0, 0.0).astype(BF16)


def _head_sum(x, ones_bd):
    hi = _bf(x)
    lo = _bf(x - hi.astype(F32))
    out = []
    for p in range(x.shape[1] // LANES):
        sl = slice(p * LANES, (p + 1) * LANES)
        out.append(_dot(hi[:, sl], ones_bd) + _dot(lo[:, sl], ones_bd))
    return jnp.concatenate(out, axis=1)


def _shift_rows_down(x, first_row):
    y = pltpu.roll(x, 1, axis=0)
    row = lax.broadcasted_iota(jnp.int32, (8, x.shape[1]), 0)
    head = jnp.where(row == 0, first_row, y[:8])
    return jnp.concatenate([head, y[8:]], axis=0)


def _shift_rows_up(x, last_row):
    n = x.shape[0]
    y = pltpu.roll(x, n - 1, axis=0)
    row = lax.broadcasted_iota(jnp.int32, (8, x.shape[1]), 0)
    tail = jnp.where(row == 7, last_row, y[n - 8:])
    return jnp.concatenate([y[:n - 8], tail], axis=0)


def _cparams(sem):
    return pltpu.CompilerParams(dimension_semantics=sem, vmem_limit_bytes=VMEM_LIMIT)


def _mod_kernel(c_ref, w_ref, b_ref, o_ref):
    c = c_ref[...]
    sc = c * _sigmoid(c)
    o_ref[0] = _dot(_bf(sc), _bf(w_ref[0])) + b_ref[0]


def _ada_mod(c, ada_w, ada_b):
    depth, d, n = ada_w.shape
    b = c.shape[0]
    tn = n // 4
    return pl.pallas_call(
        _mod_kernel,
        grid=(depth, n // tn),
        in_specs=[
            pl.BlockSpec((b, d), lambda l, j: (0, 0)),
            pl.BlockSpec((1, d, tn), lambda l, j: (l, 0, j)),
            pl.BlockSpec((1, 1, tn), lambda l, j: (l, 0, j)),
        ],
        out_specs=pl.BlockSpec((1, b, tn), lambda l, j: (l, 0, j)),
        out_shape=jax.ShapeDtypeStruct((depth, b, n), F32),
        compiler_params=_cparams(("arbitrary", "arbitrary")),
        name="ada_mod",
    )(c, ada_w, ada_b.reshape(depth, 1, n))


def _rw_proj_kernel(has_vres, x_ref, xp_ref, xn_ref, mod_ref, ng_ref, mu_ref, wrkv_ref, w0_ref, wA_ref, wB_ref,
                    a0_ref, aA_ref, aB_ref, gA_ref, gB_ref, kk_ref, ka_ref, rk_ref, *rest):
    if has_vres:
        vf_ref, v0_ref, vA_ref, vB_ref = rest[:4]
        rest = rest[4:]
    r_o, k_o, v_o, kkn_o, g_o, bonus_o, lw0_o, lw1_o, as0_o, as1_o = rest
    i = pl.program_id(1)
    nt = pl.num_programs(1)
    mod = mod_ref[0]
    shift, scale = mod[0:1], mod[1:2]
    gain = ng_ref[...]
    h = _norm_mod(x_ref[0], gain, shift, scale)
    h_before = _norm_mod(xp_ref[0], gain, shift, scale)[7:8]
    h_after = _norm_mod(xn_ref[0], gain, shift, scale)[0:1]
    h_before = jnp.where(i == 0, 0.0, h_before)
    h_after = jnp.where(i == nt - 1, 0.0, h_after)
    xx = 0.5 * (_shift_rows_down(h, h_before) + _shift_rows_up(h, h_after)) - h
    mu = mu_ref[...]
    ones_bd = _head_ones()
    ka = ka_ref[...]
    n_lora = w0_ref.shape[1] // 2

    rows = h.shape[0]
    sub = min(RW_SUB_ROWS, rows)
    for s0 in range(0, rows, sub):
        rs = slice(s0, s0 + sub)
        hs, xs = h[rs], xx[rs]
        xr, xw, xk, xv, xa, xg = (_bf(hs + xs * mu[n:n + 1]) for n in range(6))
        r = _dot(xr, wrkv_ref[0])
        k = _dot(xk, wrkv_ref[1])
        v = _dot(xv, wrkv_ref[2])
        if has_vres:
            mix = _sigmoid(v0_ref[...] + _dot(_bf(_dot(xv, vA_ref[...])), vB_ref[...]))
            v = v + (vf_ref[0, rs] - v) * mix
        r_o[0, rs] = r
        k_o[0, rs] = k
        v_o[0, rs] = v
        g_o[0, rs] = _dot(_bf(_sigmoid(_dot(xg, gA_ref[...]))), gB_ref[...])
        kk = k * kk_ref[...]
        kkn_o[0, rs] = kk * lax.rsqrt(jnp.maximum(_head_sum(kk * kk, ones_bd), 1e-24))
        wl = w0_ref[...] + _dot(_bf(jnp.tanh(_dot(xw, wA_ref[...]))), wB_ref[...])
        lw = -DECAY_CAP * _sigmoid(wl)
        a_sig = _sigmoid(a0_ref[...] + _dot(_bf(_dot(xa, aA_ref[...])), aB_ref[...]))
        lw0_o[0, rs], lw1_o[0, rs] = lw[:, :n_lora], lw[:, n_lora:]
        as0, as1 = a_sig[:, :n_lora], a_sig[:, n_lora:]
        as0_o[0, rs], as1_o[0, rs] = as0, as1
        kd_sum = k * (2.0 + (as0 + as1 - 2.0) * ka)
        bonus_o[0, rs] = _head_sum(r * kd_sum * rk_ref[...], ones_bd) * v


def _rw_proj(x, mod, ng, mu, w_rkv, w0, wA, wB, a0, aA, aB, gA, gB, k_k, k_a, r_k, vres):
    b, t, d = x.shape
    tq = min(ROW_TILE, t)
    nt = t // tq
    has_vres = vres is not None
    row = pl.BlockSpec((1, tq, d), lambda bi, i: (bi, i, 0))

    def full(a):
        nd = a.ndim
        return pl.BlockSpec(a.shape, lambda bi, i: (0,) * nd)

    halo = tq // 8
    n8 = t // 8
    in_specs = [
        row,
        pl.BlockSpec((1, 8, d), lambda bi, i: (bi, jnp.maximum(i * halo - 1, 0), 0)),
        pl.BlockSpec((1, 8, d), lambda bi, i: (bi, jnp.minimum((i + 1) * halo, n8 - 1), 0)),
        pl.BlockSpec((1, N_MOD, d), lambda bi, i: (bi, 0, 0)),
    ]
    params = [ng, mu, w_rkv, w0, wA, wB, a0, aA, aB, gA, gB, k_k, k_a, r_k]
    args = [x, x, x, mod] + params
    in_specs += [full(a) for a in params]
    if has_vres:
        v_first, v0, vA, vB = vres
        args += [v_first, v0, vA, vB]
        in_specs += [row, full(v0), full(vA), full(vB)]
    n_out = 10
    return pl.pallas_call(
        functools.partial(_rw_proj_kernel, has_vres),
        grid=(b, nt),
        in_specs=in_specs,
        out_specs=[row] * n_out,
        out_shape=[jax.ShapeDtypeStruct((b, t, d), F32)] * n_out,
        compiler_params=_cparams(("arbitrary", "arbitrary")),
        name="rw_proj",
    )(*args)


def _wkv_kernel(reverse, add_to, r_ref, k_ref, v_ref, kk_ref, lw_ref, as_ref, ka_ref, tri_ref, *rest):
    ya_ref = rest[0] if add_to else None
    y_ref, s_ref = rest[-2:]
    L = WKV_CHUNK
    rows, d = lw_ref.shape[1], lw_ref.shape[2]
    n_chunk = rows // L
    n_tile = d // LANES

    @pl.when(pl.program_id(1) == 0)
    def _():
        s_ref[...] = jnp.zeros_like(s_ref)

    lw = lw_ref[0]
    hi = _bf(lw)
    rem = lw - hi.astype(F32)
    mid = _bf(rem)
    lo = _bf(rem - mid.astype(F32))
    tri = tri_ref[...]
    cs = _dot(tri, hi) + _dot(tri, mid) + _dot(tri, lo)

    w_in = jnp.exp(cs)
    w_inv = jnp.exp(-cs)
    w_ex = w_in * jnp.exp(-lw)
    ends = [c * L if reverse else (c + 1) * L - 1 for c in range(n_chunk)]
    dec = [jnp.exp(cs[e:e + 1]) for e in ends]
    kk = kk_ref[0]
    a_sig = as_ref[0]
    v = v_ref[0]
    a_t = -(kk * w_ex)
    r_t = r_ref[0] * w_in
    b_t = kk * a_sig * w_inv
    k_t = k_ref[0] * (1.0 + (a_sig - 1.0) * ka_ref[...]) * w_inv

    lane = lax.broadcasted_iota(jnp.int32, (L, LANES), 1)
    row = lax.broadcasted_iota(jnp.int32, (L, LANES), 0)
    col = lane & (L - 1)
    head0 = lane < RW_HEAD
    if reverse:
        strict, incl = row < col, row <= col
    else:
        strict, incl = row > col, row >= col
    eye2 = jnp.where(row == col, 1.0, 0.0)
    r2 = lax.broadcasted_iota(jnp.int32, (LANES, LANES), 0) >> HEAD_SHIFT
    c2 = lax.broadcasted_iota(jnp.int32, (LANES, LANES), 1) >> HEAD_SHIFT
    same_head = r2 == c2

    def bdiag(z):
        return _bf(jnp.concatenate([jnp.where(head0, z, 0.0), jnp.where(head0, 0.0, z)], axis=0))

    chunks = list(range(n_chunk - 1, -1, -1) if reverse else range(n_chunk))
    group_chunks = [chunks[g:g + WKV_GROUP] for g in range(0, n_chunk, WKV_GROUP)]
    done = {}

    def chains(g):
        pairs = [(c, p) for c in group_chunks[g] for p in range(n_tile)]
        idx = range(len(pairs))

        def cut(z):
            return [z[c * L:(c + 1) * L, p * LANES:(p + 1) * LANES] for c, p in pairs]

        def dots(lhs, w):
            m = lhs[0].shape[0]
            out = [None] * len(idx)
            n_share = min(WKV_SHARE, len(idx)) // 2 * 2
            for i in range(0, n_share, 2):
                o = _dot(jnp.concatenate([lhs[i], lhs[i + 1]], axis=0), jnp.concatenate([w[i], w[i + 1]], axis=1))
                out[i], out[i + 1] = o[:m, :LANES], o[m:, LANES:]
            for i in range(n_share, len(idx)):
                out[i] = _dot(lhs[i], w[i])
            return out

        at, rt, bt, kt, vv = (cut(z) for z in (a_t, r_t, b_t, k_t, v))
        dec_t = [dec[c][:, p * LANES:(p + 1) * LANES] for c, p in pairs]
        bh = [bt[i] * dec_t[i] for i in idx]
        kh = [kt[i] * dec_t[i] for i in idx]
        vbd = [bdiag(vv[i]) for i in idx]
        gram = [_dot(_bf(jnp.concatenate([at[i], rt[i]], axis=0)),
                     jnp.concatenate([bdiag(bt[i]).T, bdiag(kt[i]).T], axis=1)) for i in idx]
        yield
        a_ab = [jnp.where(strict, gram[i][:L, :LANES], 0.0) for i in idx]
        a_ak = [jnp.where(strict, gram[i][:L, LANES:], 0.0) for i in idx]
        a_rb = [_bf(jnp.where(incl, gram[i][L:, :LANES], 0.0)) for i in idx]
        a_rk = [jnp.where(incl, gram[i][L:, LANES:], 0.0) for i in idx]
        av = dots([_bf(jnp.concatenate([a_ak[i], a_rk[i]], axis=0)) for i in idx], vbd)
        akv = [av[i][:L] for i in idx]
        y0 = [av[i][L:] for i in idx]
        yield
        x_inv = [eye2 + jnp.where((row >> 1) == (col >> 1), a_ab[i], 0.0) for i in idx]
        lev = 1
        while (1 << lev) < L:
            sel = ((row >> (lev + 1)) == (col >> (lev + 1))) & ((row >> lev) != (col >> lev))
            step = dots([_bf(x_inv[i]) for i in idx], [bdiag(jnp.where(sel, a_ab[i], 0.0)) for i in idx])
            yield
            grow = dots([_bf(step[i]) for i in idx], [bdiag(x_inv[i]) for i in idx])
            yield
            x_inv = [x_inv[i] + grow[i] for i in idx]
            lev += 1
        sol = [_dot(_bf(x_inv[i]), jnp.concatenate([bdiag(at[i]), bdiag(akv[i])], axis=1)) for i in idx]
        yield
        at2 = [sol[i][:, :LANES] for i in idx]
        u0 = [sol[i][:, LANES:] for i in idx]
        mix = [_dot(a_rb[i], jnp.concatenate([bdiag(at2[i]), bdiag(u0[i])], axis=1)) for i in idx]
        yield
        r2_ = [_bf(rt[i] + mix[i][:, :LANES]) for i in idx]
        y0 = [y0[i] + mix[i][:, LANES:] for i in idx]
        p_bd = [_bf(jnp.where(same_head, _dot_tn(_bf(at2[i]), _bf(bh[i])), 0.0)) for i in idx]
        yield
        q_bd = [jnp.where(same_head,
                          _dot_tn(_bf(jnp.concatenate([u0[i], vv[i]], axis=0)),
                                  _bf(jnp.concatenate([bh[i], kh[i]], axis=0))), 0.0) for i in idx]
        done[g] = (pairs, r2_, y0, p_bd, q_bd, dec_t)

    def state_steps():
        state = [s_ref[p] for p in range(n_tile)]
        for g in range(len(group_chunks)):
            while g not in done:
                yield
            pairs, r2_, y0, p_bd, q_bd, dec_t = done[g]
            for i, (c, p) in enumerate(pairs):
                s0 = state[p]
                s0b = _bf(s0)
                blk = (0, slice(c * L, (c + 1) * L), slice(p * LANES, (p + 1) * LANES))
                y = _dot(r2_[i], s0b.T) + y0[i]
                y_ref[blk] = y + ya_ref[blk] if add_to else y
                state[p] = s0 * dec_t[i] + _dot(s0b, p_bd[i]) + q_bd[i]
                if p == n_tile - 1:
                    yield
        for p in range(n_tile):
            s_ref[p] = state[p]

    active = [state_steps()]
    started = 0
    tick = 0
    while active:
        if started < len(group_chunks) and tick >= started * WKV_SKEW:
            active.insert(len(active) - 1, chains(started))
            started += 1
        for gen in list(active):
            try:
                next(gen)
            except StopIteration:
                active.remove(gen)
        tick += 1


def _wkv_tri(rows, reverse):
    t = jnp.arange(rows)
    same = (t[:, None] // WKV_CHUNK) == (t[None, :] // WKV_CHUNK)
    tri = (t[:, None] <= t[None, :]) if reverse else (t[:, None] >= t[None, :])
    return (same & tri).astype(BF16)


def _wkv(r, k, v, kk, lw, a_sig, k_a, reverse, add_to=None):
    b, t, d = r.shape
    rows = min(WKV_BLOCK, t)
    nb = t // rows
    if reverse:
        blk = pl.BlockSpec((1, rows, d), lambda bi, i: (bi, nb - 1 - i, 0))
    else:
        blk = pl.BlockSpec((1, rows, d), lambda bi, i: (bi, i, 0))
    args = [r, k, v, kk, lw, a_sig, k_a, _wkv_tri(rows, reverse)]
    in_specs = [blk] * 6 + [
        pl.BlockSpec((1, d), lambda bi, i: (0, 0)),
        pl.BlockSpec((rows, rows), lambda bi, i: (0, 0)),
    ]
    aliases = {}
    if add_to is not None:
        aliases = {len(args): 0}
        args.append(add_to)
        in_specs.append(blk)
    return pl.pallas_call(
        functools.partial(_wkv_kernel, reverse, add_to is not None),
        grid=(b, nb),
        in_specs=in_specs,
        out_specs=blk,
        out_shape=jax.ShapeDtypeStruct((b, t, d), F32),
        scratch_shapes=[pltpu.VMEM((d // LANES, LANES, LANES), F32)],
        input_output_aliases=aliases,
        compiler_params=_cparams(("arbitrary", "arbitrary")),
        name="wkv_bwd" if reverse else "wkv_fwd",
    )(*args)


def _rw_out_kernel(y_ref, bonus_ref, g_ref, x_ref, mod_ref, ng_ref, lnx_ref, wo_ref, o_ref):
    ones_bd = _head_ones()
    y = y_ref[0]
    inv_n = 1.0 / RW_HEAD
    mean = _head_sum(y, ones_bd) * inv_n
    yc = y - mean
    var = _head_sum(yc * yc, ones_bd) * inv_n
    lnx = lnx_ref[...]
    yn = yc * lax.rsqrt(var + GN_EPS) * lnx[0:1] + lnx[1:2]
    o = (yn + bonus_ref[0]) * g_ref[0]
    out = _dot(_bf(o), wo_ref[...])
    gate = mod_ref[0][2:3]
    o_ref[0] = x_ref[0] + gate * _rms(out, ng_ref[...])


def _rw_out(y, bonus, g, x, mod, ng, lnx, w_o):
    b, t, d = x.shape
    tq = min(ROW_TILE, t)
    row = pl.BlockSpec((1, tq, d), lambda bi, i: (bi, i, 0))
    return pl.pallas_call(
        _rw_out_kernel,
        grid=(b, t // tq),
        in_specs=[row] * 4 + [
            pl.BlockSpec((1, N_MOD, d), lambda bi, i: (bi, 0, 0)),
            pl.BlockSpec((1, d), lambda bi, i: (0, 0)),
            pl.BlockSpec((2, d), lambda bi, i: (0, 0)),
            pl.BlockSpec((d, d), lambda bi, i: (0, 0)),
        ],
        out_specs=row,
        out_shape=jax.ShapeDtypeStruct((b, t, d), F32),
        compiler_params=_cparams(("arbitrary", "arbitrary")),
        name="rw_out",
    )(y, bonus, g, x, mod, ng, lnx, w_o)


def _rope(x, cos, sin_signed):
    lane = lax.broadcasted_iota(jnp.int32, x.shape, 1)
    partner = jnp.where((lane & 1) == 0, pltpu.roll(x, LANES - 1, axis=1), pltpu.roll(x, 1, axis=1))
    return x * cos + partner * sin_signed


def _at_qkv_kernel(x_ref, mod_ref, ng_ref, w_ref, qkg_ref, cos_ref, sin_ref, q_o, k_o, vt_o):
    mod = mod_ref[0]
    h = _norm_mod(x_ref[0], ng_ref[...], mod[0:1], mod[1:2])
    qkv = _dot(_bf(h), w_ref[...])
    nq = q_o.shape[2]
    nk = k_o.shape[2]
    cos, sin = cos_ref[...], sin_ref[...]
    qkg = qkg_ref[...]
    scale = AT_HEAD ** -0.5 * LOG2_E

    def head(z, gain, mul):
        ms = jnp.mean(z * z, axis=-1, keepdims=True)
        return _rope(z * lax.rsqrt(ms + NORM_EPS) * gain, cos, sin) * mul

    for n in range(nq // AT_HEAD):
        sl = slice(n * AT_HEAD, (n + 1) * AT_HEAD)
        q_o[0, :, sl] = _bf(head(qkv[:, sl], qkg[0:1], scale))
    for n in range(nk // AT_HEAD):
        sl = slice(n * AT_HEAD, (n + 1) * AT_HEAD)
        k_o[0, :, sl] = _bf(head(qkv[:, nq + n * AT_HEAD:nq + (n + 1) * AT_HEAD], qkg[1:2], 1.0))
    vt_o[0] = _bf(qkv[:, nq + nk:].T)


def _at_qkv(x, mod, ng, w_qkv, qk_g, cos, sin):
    b, t, d = x.shape
    tq = min(ROW_TILE, t)
    nk = AT_KVH * AT_HEAD
    row = lambda n: pl.BlockSpec((1, tq, n), lambda bi, i: (bi, i, 0))
    return pl.pallas_call(
        _at_qkv_kernel,
        grid=(b, t // tq),
        in_specs=[
            row(d),
            pl.BlockSpec((1, N_MOD, d), lambda bi, i: (bi, 0, 0)),
            pl.BlockSpec((1, d), lambda bi, i: (0, 0)),
            pl.BlockSpec(w_qkv.shape, lambda bi, i: (0, 0)),
            pl.BlockSpec((2, AT_HEAD), lambda bi, i: (0, 0)),
            pl.BlockSpec((tq, AT_HEAD), lambda bi, i: (i, 0)),
            pl.BlockSpec((tq, AT_HEAD), lambda bi, i: (i, 0)),
        ],
        out_specs=[row(d), row(nk), pl.BlockSpec((1, nk, tq), lambda bi, i: (bi, 0, i))],
        out_shape=[jax.ShapeDtypeStruct((b, t, d), BF16), jax.ShapeDtypeStruct((b, t, nk), BF16),
                   jax.ShapeDtypeStruct((b, nk, t), BF16)],
        compiler_params=_cparams(("arbitrary", "arbitrary")),
        name="at_qkv",
    )(x, mod, ng, w_qkv, qk_g, cos, sin)


def _attn_kernel(q_ref, k_ref, vt_ref, o_ref):
    group = q_ref.shape[2] // AT_HEAD
    k = k_ref[0]
    vt = jnp.concatenate([vt_ref[0], jnp.ones((BF16_ROWS, k.shape[0]), BF16)], axis=0)

    kb = min(ATT_K_BLOCK, k.shape[0])
    n_kb = k.shape[0] // kb
    blocks = [(g, b) for g in range(group) for b in range(n_kb)]

    def scores(g, b):
        return _dot_nt(k[b * kb:(b + 1) * kb], q_ref[0, :, g * AT_HEAD:(g + 1) * AT_HEAD])

    pending = [scores(*blk) for blk in blocks[:ATT_AHEAD]]
    m = acc = None
    for i, (g, b) in enumerate(blocks):
        if i + ATT_AHEAD < len(blocks):
            pending.append(scores(*blocks[i + ATT_AHEAD]))
        st = pending.pop(0)
        m_blk = jnp.max(st, axis=0, keepdims=True)
        if b == 0:
            m = m_blk
        else:
            m_new = jnp.maximum(m, m_blk)
            acc = acc * jnp.exp2(m - m_new)
            m = m_new
        part = _dot(vt[:, b * kb:(b + 1) * kb], jnp.exp2(_bf(st - m)))
        acc = part if b == 0 else acc + part
        if b == n_kb - 1:
            o_ref[0, :, g * AT_HEAD:(g + 1) * AT_HEAD] = _bf((acc[:AT_HEAD] / acc[AT_HEAD:AT_HEAD + 1]).T)


def _attention(q, k, vt):
    b, t, d = q.shape
    group = d // AT_HEAD // AT_KVH
    gw = group * AT_HEAD
    tq = min(ATT_Q_TILE, t)
    qblk = pl.BlockSpec((1, tq, gw), lambda bi, h, i: (bi, i, h))
    return pl.pallas_call(
        _attn_kernel,
        grid=(b, AT_KVH, t // tq),
        in_specs=[qblk,
                  pl.BlockSpec((1, t, AT_HEAD), lambda bi, h, i: (bi, 0, h)),
                  pl.BlockSpec((1, AT_HEAD, t), lambda bi, h, i: (bi, h, 0))],
        out_specs=qblk,
        out_shape=jax.ShapeDtypeStruct((b, t, d), BF16),
        compiler_params=_cparams(("arbitrary", "arbitrary", "arbitrary")),
        name="attention",
    )(q, k, vt)


def _at_out_kernel(a_ref, x_ref, mod_ref, ng_ref, wo_ref, o_ref):
    out = _dot(a_ref[0], wo_ref[...])
    gate = mod_ref[0][2:3]
    o_ref[0] = x_ref[0] + gate * _rms(out, ng_ref[...])


def _at_out(a, x, mod, ng, w_o):
    b, t, d = x.shape
    tq = min(ROW_TILE, t)
    row = pl.BlockSpec((1, tq, d), lambda bi, i: (bi, i, 0))
    return pl.pallas_call(
        _at_out_kernel,
        grid=(b, t // tq),
        in_specs=[row, row,
                  pl.BlockSpec((1, N_MOD, d), lambda bi, i: (bi, 0, 0)),
                  pl.BlockSpec((1, d), lambda bi, i: (0, 0)),
                  pl.BlockSpec((d, d), lambda bi, i: (0, 0))],
        out_specs=row,
        out_shape=jax.ShapeDtypeStruct((b, t, d), F32),
        compiler_params=_cparams(("arbitrary", "arbitrary")),
        name="at_out",
    )(a, x, mod, ng, w_o)


def _ffn_kernel(x_ref, xp_ref, xn_ref, mod_ref, ng_ref, wu_ref, cw_ref, cb_ref, wd_ref, o_ref):
    i = pl.program_id(1)
    nt = pl.num_programs(1)
    rows = x_ref.shape[1]
    f_dim = wd_ref.shape[0]
    ft = min(FFN_TILE, f_dim)
    mod = mod_ref[0]
    ng = ng_ref[...]
    x = x_ref[0]
    h_all = _bf(_norm_mod(jnp.concatenate([xp_ref[0], x, xn_ref[0]], axis=0), ng[0:1], mod[3:4], mod[4:5]))
    n_f = f_dim // ft
    sub = min(FFN_SUB_ROWS, rows)
    n_sub = rows // sub
    top, bot = BF16_ROWS, BF16_ROWS + sub

    for s in range(n_sub):
        h = h_all[s * sub:s * sub + sub + 2 * BF16_ROWS]
        pad_top = (i == 0) if s == 0 else False
        pad_bot = (i == nt - 1) if s == n_sub - 1 else False

        def up(f, h=h):
            cols = slice(f * ft, (f + 1) * ft)
            return _dot(h, wu_ref[0, :, cols]), _dot(h[top:bot], wu_ref[1, :, cols])

        pending = [up(f) for f in range(min(FFN_AHEAD, n_f))]
        acc = None
        for f in range(n_f):
            if f + FFN_AHEAD < n_f:
                pending.append(up(f + FFN_AHEAD))
            u, gv = pending.pop(0)
            cols = slice(f * ft, (f + 1) * ft)
            u = jnp.concatenate([jnp.where(pad_top, 0.0, u[:top]), u[top:bot], jnp.where(pad_bot, 0.0, u[bot:])], axis=0)
            cw = cw_ref[:, cols]
            uc = (cw[0:1] * pltpu.roll(u, 1, axis=0)[top:bot] + cw[1:2] * u[top:bot]
                  + cw[2:3] * pltpu.roll(u, u.shape[0] - 1, axis=0)[top:bot] + cb_ref[:, cols])
            act = uc * _sigmoid(uc) * gv
            part = _dot(_bf(act), wd_ref[cols, :])
            acc = part if acc is None else acc + part
        rs = slice(s * sub, (s + 1) * sub)
        o_ref[0, rs] = x[rs] + mod[5:6] * _rms(acc, ng[1:2])


def _ffn(x, mod, ng, w_up, conv_w, conv_b, w_down):
    b, t, d = x.shape
    f_dim = w_down.shape[0]
    rows = min(FFN_ROWS, t)
    halo = rows // BF16_ROWS
    n_halo = t // BF16_ROWS

    def full(a):
        nd = a.ndim
        return pl.BlockSpec(a.shape, lambda bi, i: (0,) * nd)

    conv_b = conv_b.reshape(1, f_dim)
    row = pl.BlockSpec((1, rows, d), lambda bi, i: (bi, i, 0))
    return pl.pallas_call(
        _ffn_kernel,
        grid=(b, t // rows),
        in_specs=[
            row,
            pl.BlockSpec((1, BF16_ROWS, d), lambda bi, i: (bi, jnp.maximum(i * halo - 1, 0), 0)),
            pl.BlockSpec((1, BF16_ROWS, d), lambda bi, i: (bi, jnp.minimum((i + 1) * halo, n_halo - 1), 0)),
            pl.BlockSpec((1, N_MOD, d), lambda bi, i: (bi, 0, 0)),
            full(ng), full(w_up), full(conv_w), full(conv_b), full(w_down),
        ],
        out_specs=row,
        out_shape=jax.ShapeDtypeStruct((b, t, d), F32),
        compiler_params=_cparams(("arbitrary", "arbitrary")),
        name="ffn",
    )(x, x, x, mod, ng, w_up, conv_w, conv_b, w_down)


def _rope_tables(t):
    pos = jnp.arange(t)
    row = (pos // GRID_W).astype(F32)
    col = (pos % GRID_W).astype(F32)
    n_pair = AT_HEAD // 4
    inv = ROPE_THETA ** (-jnp.arange(n_pair, dtype=F32) / n_pair)
    ang = jnp.concatenate([row[:, None] * inv, col[:, None] * inv], axis=-1)
    cos = jnp.repeat(jnp.cos(ang), 2, axis=-1)
    sin = jnp.repeat(jnp.sin(ang), 2, axis=-1)
    sign = jnp.where(jnp.arange(AT_HEAD) % 2 == 0, -1.0, 1.0)
    return cos, sin * sign


def _trunk(x, mod_all, cos, sin, norm_g, rw, at, ffn):
    v_first = None
    for l in range(mod_all.shape[0]):
        mod = mod_all[l]
        ng = norm_g[l]
        j = l // 2
        if l % 2 == 0:
            proj, vres, ka, lnx, w_o = rw[j]
            if vres is not None:
                vres = (v_first,) + vres
            r, k, v, kk, g, bonus, lw0, lw1, as0, as1 = _rw_proj(x, mod, ng[0:1], *proj, vres)
            if j == 0:
                v_first = v
            y = _wkv(r, k, v, kk, lw0, as0, ka, reverse=False)
            y = _wkv(r, k, v, kk, lw1, as1, ka, reverse=True, add_to=y)
            x = _rw_out(y, bonus, g, x, mod, ng[1:2], lnx, w_o)
        else:
            w_qkv, qk_g, w_o = at[j]
            q, k, vt = _at_qkv(x, mod, ng[0:1], w_qkv, qk_g, cos, sin)
            x = _at_out(_attention(q, k, vt), x, mod, ng[1:2], w_o)
        x = _ffn(x, mod, ng[2:4], *ffn[l])
    return x


def kernel(x_prompt, x_sample, c_prompt, c_sample, ada_w, ada_b, norm_g, rw_mu, rw_rkv, rw_w0, rw_wA, rw_wB, rw_a0,
           rw_aA, rw_aB, rw_gA, rw_gB, rw_kk, rw_ka, rw_rk, rw_lnx, rw_o, rw_v0, rw_vA, rw_vB, at_qkv, at_qk_g,
           at_o, ffn_up, ffn_conv_w, ffn_conv_b, ffn_down):
    n_prompt = x_prompt.shape[0]
    d = x_prompt.shape[2]
    depth = ada_w.shape[0]
    f_dim = ffn_down.shape[1]
    row = lambda z: z.reshape(1, d)

    def both_dirs(bias, w_in, w_out):
        zero = jnp.zeros_like(w_out[0])
        w_bd = jnp.concatenate([jnp.concatenate([w_out[0], zero], axis=1),
                                jnp.concatenate([zero, w_out[1]], axis=1)], axis=0)
        return bias.reshape(1, 2 * d), _bf(jnp.concatenate([w_in[0], w_in[1]], axis=1)), _bf(w_bd)

    rw = []
    for j in range(rw_rkv.shape[0]):
        proj = (rw_mu[j], _bf(rw_rkv[j]), *both_dirs(rw_w0[j], rw_wA[j], rw_wB[j]),
                *both_dirs(rw_a0[j], rw_aA[j], rw_aB[j]), _bf(rw_gA[j]), _bf(rw_gB[j]), row(rw_kk[j]),
                row(rw_ka[j]), row(rw_rk[j]))
        vres = None if j == 0 else (row(rw_v0[j - 1]), _bf(rw_vA[j - 1]), _bf(rw_vB[j - 1]))
        rw.append((proj, vres, row(rw_ka[j]), rw_lnx[j], _bf(rw_o[j])))
    at = [(_bf(at_qkv[j]), at_qk_g[j], _bf(at_o[j])) for j in range(at_qkv.shape[0])]
    ffn = [(_bf(ffn_up[l]).reshape(d, 2, f_dim).transpose(1, 0, 2), ffn_conv_w[l], ffn_conv_b[l], _bf(ffn_down[l]))
           for l in range(depth)]

    c = jnp.concatenate([c_prompt, c_sample], axis=0)
    mod_all = _ada_mod(c, ada_w, ada_b).reshape(depth, c.shape[0], N_MOD, d)
    outs = []
    for x, mod in ((x_prompt, mod_all[:, :n_prompt]), (x_sample, mod_all[:, n_prompt:])):
        cos, sin = _rope_tables(x.shape[1])
        outs.append(_trunk(x, mod, cos, sin, norm_g, rw, at, ffn))
    return tuple(outs)
```

```python
import functools

import jax
import jax.numpy as jnp
from jax import lax
from jax.experimental import pallas as pl
from jax.experimental.pallas import tpu as pltpu

F32 = jnp.float32
BF16 = jnp.bfloat16

RW_HEAD = 64
AT_HEAD = 128
AT_KVH = 2
GRID_W = 64
ROPE_THETA = 10000.0
NORM_EPS = 1e-6
GN_EPS = 64e-5
N_MOD = 6

LOG2_E = 1.4426950408889634
DECAY_CAP = 0.6065306597126334
LANES = 128
BF16_ROWS = 16
HEAD_SHIFT = RW_HEAD.bit_length() - 1
WKV_CHUNK = 64
WKV_BLOCK = 256
ROW_TILE = 256
ATT_Q_TILE = 256
ATT_K_BLOCK = 256
ATT_AHEAD = 8
FFN_TILE = 256
FFN_ROWS = 512
FFN_SUB_ROWS = 256
FFN_AHEAD = 2
VMEM_LIMIT = 56 * 1024 * 1024


def _dot(a, b):
    return jnp.dot(a, b, preferred_element_type=F32)


def _dot_nt(a, b):
    return lax.dot_general(a, b, (((1,), (1,)), ((), ())), preferred_element_type=F32)


def _dot_tn(a, b):
    return lax.dot_general(a, b, (((0,), (0,)), ((), ())), preferred_element_type=F32)


def _bf(x):
    return x.astype(BF16)


def _sigmoid(x):
    return 1.0 / (1.0 + jnp.exp(-x))


def _norm_mod(x, gain, shift, scale):
    ms = jnp.mean(x * x, axis=-1, keepdims=True)
    return (x * lax.rsqrt(ms + NORM_EPS)) * (gain * (1.0 + scale)) + shift


def _rms(x, gain):
    ms = jnp.mean(x * x, axis=-1, keepdims=True)
    return x * lax.rsqrt(ms + NORM_EPS) * gain


def _head_ones():
    r = lax.broadcasted_iota(jnp.int32, (LANES, LANES), 0) >> HEAD_SHIFT
    c = lax.broadcasted_iota(jnp.int32, (LANES, LANES), 1) >> HEAD_SHIFT
    return jnp.where(r == c, 1.0, 0.0).astype(BF16)


def _head_sum(x, ones_bd):
    hi = _bf(x)
    lo = _bf(x - hi.astype(F32))
    out = []
    for p in range(x.shape[1] // LANES):
        sl = slice(p * LANES, (p + 1) * LANES)
        out.append(_dot(hi[:, sl], ones_bd) + _dot(lo[:, sl], ones_bd))
    return jnp.concatenate(out, axis=1)


def _shift_rows_down(x, first_row):
    y = pltpu.roll(x, 1, axis=0)
    row = lax.broadcasted_iota(jnp.int32, (8, x.shape[1]), 0)
    head = jnp.where(row == 0, first_row, y[:8])
    return jnp.concatenate([head, y[8:]], axis=0)


def _shift_rows_up(x, last_row):
    n = x.shape[0]
    y = pltpu.roll(x, n - 1, axis=0)
    row = lax.broadcasted_iota(jnp.int32, (8, x.shape[1]), 0)
    tail = jnp.where(row == 7, last_row, y[n - 8:])
    return jnp.concatenate([y[:n - 8], tail], axis=0)


def _cparams(sem):
    return pltpu.CompilerParams(dimension_semantics=sem, vmem_limit_bytes=VMEM_LIMIT)


def _mod_kernel(c_ref, w_ref, b_ref, o_ref):
    c = c_ref[...]
    sc = c * _sigmoid(c)
    o_ref[0] = _dot(_bf(sc), _bf(w_ref[0])) + b_ref[0]


def _ada_mod(c, ada_w, ada_b):
    depth, d, n = ada_w.shape
    b = c.shape[0]
    tn = n // 4
    return pl.pallas_call(
        _mod_kernel,
        grid=(depth, n // tn),
        in_specs=[
            pl.BlockSpec((b, d), lambda l, j: (0, 0)),
            pl.BlockSpec((1, d, tn), lambda l, j: (l, 0, j)),
            pl.BlockSpec((1, 1, tn), lambda l, j: (l, 0, j)),
        ],
        out_specs=pl.BlockSpec((1, b, tn), lambda l, j: (l, 0, j)),
        out_shape=jax.ShapeDtypeStruct((depth, b, n), F32),
        compiler_params=_cparams(("arbitrary", "arbitrary")),
        name="ada_mod",
    )(c, ada_w, ada_b.reshape(depth, 1, n))


def _rw_proj_kernel(has_vres, x_ref, xp_ref, xn_ref, mod_ref, ng_ref, mu_ref, wrkv_ref, w0_ref, wA_ref, wB_ref,
                    a0_ref, aA_ref, aB_ref, gA_ref, gB_ref, kk_ref, ka_ref, rk_ref, *rest):
    if has_vres:
        vf_ref, v0_ref, vA_ref, vB_ref = rest[:4]
        rest = rest[4:]
    r_o, k_o, v_o, kkn_o, g_o, bonus_o, lw0_o, lw1_o, as0_o, as1_o = rest
    i = pl.program_id(1)
    nt = pl.num_programs(1)
    mod = mod_ref[0]
    shift, scale = mod[0:1], mod[1:2]
    gain = ng_ref[...]
    h = _norm_mod(x_ref[0], gain, shift, scale)
    h_before = _norm_mod(xp_ref[0], gain, shift, scale)[7:8]
    h_after = _norm_mod(xn_ref[0], gain, shift, scale)[0:1]
    h_before = jnp.where(i == 0, 0.0, h_before)
    h_after = jnp.where(i == nt - 1, 0.0, h_after)
    xx = 0.5 * (_shift_rows_down(h, h_before) + _shift_rows_up(h, h_after)) - h
    mu = mu_ref[...]
    ones_bd = _head_ones()
    n_lora = w0_ref.shape[1] // 2

    xr, xw, xk, xv, xa, xg = (_bf(h + xx * mu[n:n + 1]) for n in range(6))
    r = _dot(xr, wrkv_ref[0])
    k = _dot(xk, wrkv_ref[1])
    v = _dot(xv, wrkv_ref[2])
    if has_vres:
        mix = _sigmoid(v0_ref[...] + _dot(_bf(_dot(xv, vA_ref[...])), vB_ref[...]))
        v = v + (vf_ref[0] - v) * mix
    r_o[0] = r
    k_o[0] = k
    v_o[0] = v
    g_o[0] = _dot(_bf(_sigmoid(_dot(xg, gA_ref[...]))), gB_ref[...])
    kk = k * kk_ref[...]
    kkn_o[0] = kk * lax.rsqrt(jnp.maximum(_head_sum(kk * kk, ones_bd), 1e-24))
    wl = w0_ref[...] + _dot(_bf(jnp.tanh(_dot(xw, wA_ref[...]))), wB_ref[...])
    lw = -DECAY_CAP * _sigmoid(wl)
    a_sig = _sigmoid(a0_ref[...] + _dot(_bf(_dot(xa, aA_ref[...])), aB_ref[...]))
    lw0_o[0], lw1_o[0] = lw[:, :n_lora], lw[:, n_lora:]
    as0, as1 = a_sig[:, :n_lora], a_sig[:, n_lora:]
    as0_o[0], as1_o[0] = as0, as1
    kd_sum = k * (2.0 + (as0 + as1 - 2.0) * ka_ref[...])
    bonus_o[0] = _head_sum(r * kd_sum * rk_ref[...], ones_bd) * v


def _rw_proj(x, mod, ng, mu, w_rkv, w0, wA, wB, a0, aA, aB, gA, gB, k_k, k_a, r_k, vres):
    b, t, d = x.shape
    tq = min(ROW_TILE, t)
    nt = t // tq
    has_vres = vres is not None
    row = pl.BlockSpec((1, tq, d), lambda bi, i: (bi, i, 0))

    def full(a):
        nd = a.ndim
        return pl.BlockSpec(a.shape, lambda bi, i: (0,) * nd)

    halo = tq // 8
    n8 = t // 8
    in_specs = [
        row,
        pl.BlockSpec((1, 8, d), lambda bi, i: (bi, jnp.maximum(i * halo - 1, 0), 0)),
        pl.BlockSpec((1, 8, d), lambda bi, i: (bi, jnp.minimum((i + 1) * halo, n8 - 1), 0)),
        pl.BlockSpec((1, N_MOD, d), lambda bi, i: (bi, 0, 0)),
    ]
    params = [ng, mu, w_rkv, w0, wA, wB, a0, aA, aB, gA, gB, k_k, k_a, r_k]
    args = [x, x, x, mod] + params
    in_specs += [full(a) for a in params]
    if has_vres:
        v_first, v0, vA, vB = vres
        args += [v_first, v0, vA, vB]
        in_specs += [row, full(v0), full(vA), full(vB)]
    n_out = 10
    return pl.pallas_call(
        functools.partial(_rw_proj_kernel, has_vres),
        grid=(b, nt),
        in_specs=in_specs,
        out_specs=[row] * n_out,
        out_shape=[jax.ShapeDtypeStruct((b, t, d), F32)] * n_out,
        compiler_params=_cparams(("arbitrary", "arbitrary")),
        name="rw_proj",
    )(*args)


def _wkv_kernel(reverse, finish, r_ref, k_ref, v_ref, kk_ref, lw_ref, as_ref, ka_ref, tri_ref, *rest):
    if finish:
        ya_ref, bonus_ref, g_ref, x_ref, mod_ref, ng_ref, lnx_ref, wo_ref, o_ref, s_ref = rest
    else:
        o_ref, s_ref = rest
    L = WKV_CHUNK
    rows, d = lw_ref.shape[1], lw_ref.shape[2]
    n_chunk = rows // L
    n_tile = d // LANES

    @pl.when(pl.program_id(1) == 0)
    def _():
        s_ref[...] = jnp.zeros_like(s_ref)

    lw = lw_ref[0]
    hi = _bf(lw)
    rem = lw - hi.astype(F32)
    mid = _bf(rem)
    lo = _bf(rem - mid.astype(F32))
    tri = tri_ref[...]
    cs = _dot(tri, hi) + _dot(tri, mid) + _dot(tri, lo)

    w_in = jnp.exp(cs)
    w_inv = jnp.exp(-cs)
    w_ex = w_in * jnp.exp(-lw)
    ends = [c * L if reverse else (c + 1) * L - 1 for c in range(n_chunk)]
    dec = [jnp.exp(cs[e:e + 1]) for e in ends]
    kk = kk_ref[0]
    a_sig = as_ref[0]
    v = v_ref[0]
    a_t = -(kk * w_ex)
    r_t = r_ref[0] * w_in
    b_t = kk * a_sig * w_inv
    k_t = k_ref[0] * (1.0 + (a_sig - 1.0) * ka_ref[...]) * w_inv

    lane = lax.broadcasted_iota(jnp.int32, (L, LANES), 1)
    row = lax.broadcasted_iota(jnp.int32, (L, LANES), 0)
    col = lane & (L - 1)
    head0 = lane < RW_HEAD
    if reverse:
        strict, incl = row < col, row <= col
    else:
        strict, incl = row > col, row >= col
    eye2 = jnp.where(row == col, 1.0, 0.0)
    r2 = lax.broadcasted_iota(jnp.int32, (LANES, LANES), 0) >> HEAD_SHIFT
    c2 = lax.broadcasted_iota(jnp.int32, (LANES, LANES), 1) >> HEAD_SHIFT
    same_head = r2 == c2

    def bdiag(z):
        return _bf(jnp.concatenate([jnp.where(head0, z, 0.0), jnp.where(head0, 0.0, z)], axis=0))

    chunks = list(range(n_chunk - 1, -1, -1) if reverse else range(n_chunk))
    pairs = [(c, p) for c in chunks for p in range(n_tile)]
    idx = range(len(pairs))

    def cut(z):
        return [z[c * L:(c + 1) * L, p * LANES:(p + 1) * LANES] for c, p in pairs]

    def dots(lhs, w):
        return [_dot(lhs[i], w[i]) for i in idx]

    at, rt, bt, kt, vv = (cut(z) for z in (a_t, r_t, b_t, k_t, v))
    dec_t = [dec[c][:, p * LANES:(p + 1) * LANES] for c, p in pairs]
    bh = [bt[i] * dec_t[i] for i in idx]
    kh = [kt[i] * dec_t[i] for i in idx]
    vbd = [bdiag(vv[i]) for i in idx]
    gram = dots([_bf(jnp.concatenate([at[i], rt[i]], axis=0)) for i in idx],
                [jnp.concatenate([bdiag(bt[i]).T, bdiag(kt[i]).T], axis=1) for i in idx])
    a_ab = [jnp.where(strict, gram[i][:L, :LANES], 0.0) for i in idx]
    a_ak = [jnp.where(strict, gram[i][:L, LANES:], 0.0) for i in idx]
    a_rb = [_bf(jnp.where(incl, gram[i][L:, :LANES], 0.0)) for i in idx]
    a_rk = [jnp.where(incl, gram[i][L:, LANES:], 0.0) for i in idx]
    av = dots([_bf(jnp.concatenate([a_ak[i], a_rk[i]], axis=0)) for i in idx], vbd)
    akv = [av[i][:L] for i in idx]
    y0 = [av[i][L:] for i in idx]
    x_inv = [eye2 + jnp.where((row >> 1) == (col >> 1), a_ab[i], 0.0) for i in idx]
    lev = 1
    while (1 << lev) < L:
        sel = ((row >> (lev + 1)) == (col >> (lev + 1))) & ((row >> lev) != (col >> lev))
        step = dots([_bf(x_inv[i]) for i in idx], [bdiag(jnp.where(sel, a_ab[i], 0.0)) for i in idx])
        grow = dots([_bf(step[i]) for i in idx], [bdiag(x_inv[i]) for i in idx])
        x_inv = [x_inv[i] + grow[i] for i in idx]
        lev += 1
    sol = dots([_bf(x_inv[i]) for i in idx],
               [jnp.concatenate([bdiag(at[i]), bdiag(akv[i])], axis=1) for i in idx])
    at2 = [sol[i][:, :LANES] for i in idx]
    u0 = [sol[i][:, LANES:] for i in idx]
    mix = dots(a_rb, [jnp.concatenate([bdiag(at2[i]), bdiag(u0[i])], axis=1) for i in idx])
    r2_ = [_bf(rt[i] + mix[i][:, :LANES]) for i in idx]
    y0 = [y0[i] + mix[i][:, LANES:] for i in idx]
    p_bd = [_bf(jnp.where(same_head, _dot_tn(_bf(at2[i]), _bf(bh[i])), 0.0)) for i in idx]
    q_bd = [jnp.where(same_head,
                      _dot_tn(_bf(jnp.concatenate([u0[i], vv[i]], axis=0)),
                              _bf(jnp.concatenate([bh[i], kh[i]], axis=0))), 0.0) for i in idx]
    state = [s_ref[p] for p in range(n_tile)]
    y = {}
    for i, (c, p) in enumerate(pairs):
        s0 = state[p]
        s0b = _bf(s0)
        y[c, p] = _dot(r2_[i], s0b.T) + y0[i]
        state[p] = s0 * dec_t[i] + _dot(s0b, p_bd[i]) + q_bd[i]
    for p in range(n_tile):
        s_ref[p] = state[p]
    y = jnp.concatenate([jnp.concatenate([y[c, p] for p in range(n_tile)], axis=1) for c in range(n_chunk)], axis=0)
    if finish:
        o_ref[0] = _rw_finish(y + ya_ref[0], bonus_ref[0], g_ref[0], x_ref[0], mod_ref[0][2:3], ng_ref[...],
                              lnx_ref[...], wo_ref[...])
    else:
        o_ref[0] = y


def _rw_finish(y, bonus, g, x, gate, ng, lnx, w_o):
    ones_bd = _head_ones()
    inv_n = 1.0 / RW_HEAD
    mean = _head_sum(y, ones_bd) * inv_n
    yc = y - mean
    var = _head_sum(yc * yc, ones_bd) * inv_n
    yn = yc * lax.rsqrt(var + GN_EPS) * lnx[0:1] + lnx[1:2]
    out = _dot(_bf((yn + bonus) * g), w_o)
    return x + gate * _rms(out, ng)


def _wkv_tri(rows, reverse):
    t = jnp.arange(rows)
    same = (t[:, None] // WKV_CHUNK) == (t[None, :] // WKV_CHUNK)
    tri = (t[:, None] <= t[None, :]) if reverse else (t[:, None] >= t[None, :])
    return (same & tri).astype(BF16)


def _wkv(r, k, v, kk, lw, a_sig, k_a, reverse, finish=None):
    b, t, d = r.shape
    rows = min(WKV_BLOCK, t)
    nb = t // rows
    if reverse:
        blk = pl.BlockSpec((1, rows, d), lambda bi, i: (bi, nb - 1 - i, 0))
    else:
        blk = pl.BlockSpec((1, rows, d), lambda bi, i: (bi, i, 0))

    def full(a):
        nd = a.ndim
        return pl.BlockSpec(a.shape, lambda bi, i: (0,) * nd)

    tri = _wkv_tri(rows, reverse)
    args = [r, k, v, kk, lw, a_sig, k_a, tri]
    in_specs = [blk] * 6 + [full(k_a), full(tri)]
    if finish is not None:
        y_other, bonus, g, x, mod, ng, lnx, w_o = finish
        args += [y_other, bonus, g, x, mod, ng, lnx, w_o]
        in_specs += [blk] * 4 + [pl.BlockSpec((1, N_MOD, d), lambda bi, i: (bi, 0, 0)), full(ng), full(lnx), full(w_o)]
    return pl.pallas_call(
        functools.partial(_wkv_kernel, reverse, finish is not None),
        grid=(b, nb),
        in_specs=in_specs,
        out_specs=blk,
        out_shape=jax.ShapeDtypeStruct((b, t, d), F32),
        scratch_shapes=[pltpu.VMEM((d // LANES, LANES, LANES), F32)],
        compiler_params=_cparams(("arbitrary", "arbitrary")),
        name="wkv_bwd" if reverse else "wkv_fwd",
    )(*args)


def _rope(x, cos, sin_signed):
    lane = lax.broadcasted_iota(jnp.int32, x.shape, 1)
    partner = jnp.where((lane & 1) == 0, pltpu.roll(x, LANES - 1, axis=1), pltpu.roll(x, 1, axis=1))
    return x * cos + partner * sin_signed


def _at_qkv_kernel(x_ref, mod_ref, ng_ref, w_ref, qkg_ref, cos_ref, sin_ref, q_o, k_o, vt_o):
    mod = mod_ref[0]
    h = _norm_mod(x_ref[0], ng_ref[...], mod[0:1], mod[1:2])
    qkv = _dot(_bf(h), w_ref[...])
    nq = q_o.shape[2]
    nk = k_o.shape[2]
    cos, sin = cos_ref[...], sin_ref[...]
    qkg = qkg_ref[...]
    scale = AT_HEAD ** -0.5 * LOG2_E

    def head(z, gain, mul):
        ms = jnp.mean(z * z, axis=-1, keepdims=True)
        return _rope(z * lax.rsqrt(ms + NORM_EPS) * gain, cos, sin) * mul

    for n in range(nq // AT_HEAD):
        sl = slice(n * AT_HEAD, (n + 1) * AT_HEAD)
        q_o[0, :, sl] = _bf(head(qkv[:, sl], qkg[0:1], scale))
    for n in range(nk // AT_HEAD):
        sl = slice(n * AT_HEAD, (n + 1) * AT_HEAD)
        k_o[0, :, sl] = _bf(head(qkv[:, nq + n * AT_HEAD:nq + (n + 1) * AT_HEAD], qkg[1:2], 1.0))
    vt_o[0] = _bf(qkv[:, nq + nk:].T)


def _at_qkv(x, mod, ng, w_qkv, qk_g, cos, sin):
    b, t, d = x.shape
    tq = min(ROW_TILE, t)
    nk = AT_KVH * AT_HEAD
    row = lambda n: pl.BlockSpec((1, tq, n), lambda bi, i: (bi, i, 0))
    return pl.pallas_call(
        _at_qkv_kernel,
        grid=(b, t // tq),
        in_specs=[
            row(d),
            pl.BlockSpec((1, N_MOD, d), lambda bi, i: (bi, 0, 0)),
            pl.BlockSpec((1, d), lambda bi, i: (0, 0)),
            pl.BlockSpec(w_qkv.shape, lambda bi, i: (0, 0)),
            pl.BlockSpec((2, AT_HEAD), lambda bi, i: (0, 0)),
            pl.BlockSpec((tq, AT_HEAD), lambda bi, i: (i, 0)),
            pl.BlockSpec((tq, AT_HEAD), lambda bi, i: (i, 0)),
        ],
        out_specs=[row(d), row(nk), pl.BlockSpec((1, nk, tq), lambda bi, i: (bi, 0, i))],
        out_shape=[jax.ShapeDtypeStruct((b, t, d), BF16), jax.ShapeDtypeStruct((b, t, nk), BF16),
                   jax.ShapeDtypeStruct((b, nk, t), BF16)],
        compiler_params=_cparams(("arbitrary", "arbitrary")),
        name="at_qkv",
    )(x, mod, ng, w_qkv, qk_g, cos, sin)


def _attn_kernel(q_ref, k_ref, vt_ref, o_ref):
    group = q_ref.shape[2] // AT_HEAD
    k = k_ref[0]
    vt = jnp.concatenate([vt_ref[0], jnp.ones((BF16_ROWS, k.shape[0]), BF16)], axis=0)

    kb = min(ATT_K_BLOCK, k.shape[0])
    n_kb = k.shape[0] // kb
    blocks = [(g, b) for g in range(group) for b in range(n_kb)]

    def scores(g, b):
        return _dot_nt(k[b * kb:(b + 1) * kb], q_ref[0, :, g * AT_HEAD:(g + 1) * AT_HEAD])

    pending = [scores(*blk) for blk in blocks[:ATT_AHEAD]]
    m = acc = None
    for i, (g, b) in enumerate(blocks):
        if i + ATT_AHEAD < len(blocks):
            pending.append(scores(*blocks[i + ATT_AHEAD]))
        st = pending.pop(0)
        m_blk = jnp.max(st, axis=0, keepdims=True)
        if b == 0:
            m = m_blk
        else:
            m_new = jnp.maximum(m, m_blk)
            acc = acc * jnp.exp2(m - m_new)
            m = m_new
        part = _dot(vt[:, b * kb:(b + 1) * kb], jnp.exp2(_bf(st - m)))
        acc = part if b == 0 else acc + part
        if b == n_kb - 1:
            o_ref[0, :, g * AT_HEAD:(g + 1) * AT_HEAD] = _bf((acc[:AT_HEAD] / acc[AT_HEAD:AT_HEAD + 1]).T)


def _attention(q, k, vt):
    b, t, d = q.shape
    group = d // AT_HEAD // AT_KVH
    gw = group * AT_HEAD
    tq = min(ATT_Q_TILE, t)
    qblk = pl.BlockSpec((1, tq, gw), lambda bi, h, i: (bi, i, h))
    return pl.pallas_call(
        _attn_kernel,
        grid=(b, AT_KVH, t // tq),
        in_specs=[qblk,
                  pl.BlockSpec((1, t, AT_HEAD), lambda bi, h, i: (bi, 0, h)),
                  pl.BlockSpec((1, AT_HEAD, t), lambda bi, h, i: (bi, h, 0))],
        out_specs=qblk,
        out_shape=jax.ShapeDtypeStruct((b, t, d), BF16),
        compiler_params=_cparams(("arbitrary", "arbitrary", "arbitrary")),
        name="attention",
    )(q, k, vt)


def _at_out_kernel(a_ref, x_ref, mod_ref, ng_ref, wo_ref, o_ref):
    out = _dot(a_ref[0], wo_ref[...])
    gate = mod_ref[0][2:3]
    o_ref[0] = x_ref[0] + gate * _rms(out, ng_ref[...])


def _at_out(a, x, mod, ng, w_o):
    b, t, d = x.shape
    tq = min(ROW_TILE, t)
    row = pl.BlockSpec((1, tq, d), lambda bi, i: (bi, i, 0))
    return pl.pallas_call(
        _at_out_kernel,
        grid=(b, t // tq),
        in_specs=[row, row,
                  pl.BlockSpec((1, N_MOD, d), lambda bi, i: (bi, 0, 0)),
                  pl.BlockSpec((1, d), lambda bi, i: (0, 0)),
                  pl.BlockSpec((d, d), lambda bi, i: (0, 0))],
        out_specs=row,
        out_shape=jax.ShapeDtypeStruct((b, t, d), F32),
        compiler_params=_cparams(("arbitrary", "arbitrary")),
        name="at_out",
    )(a, x, mod, ng, w_o)


def _ffn_kernel(x_ref, xp_ref, xn_ref, mod_ref, ng_ref, wu_ref, cw_ref, cb_ref, wd_ref, o_ref):
    i = pl.program_id(1)
    nt = pl.num_programs(1)
    rows = x_ref.shape[1]
    f_dim = wd_ref.shape[0]
    ft = min(FFN_TILE, f_dim)
    mod = mod_ref[0]
    ng = ng_ref[...]
    x = x_ref[0]
    h_all = _bf(_norm_mod(jnp.concatenate([xp_ref[0], x, xn_ref[0]], axis=0), ng[0:1], mod[3:4], mod[4:5]))
    n_f = f_dim // ft
    sub = min(FFN_SUB_ROWS, rows)
    n_sub = rows // sub
    top, bot = BF16_ROWS, BF16_ROWS + sub

    for s in range(n_sub):
        h = h_all[s * sub:s * sub + sub + 2 * BF16_ROWS]
        pad_top = (i == 0) if s == 0 else False
        pad_bot = (i == nt - 1) if s == n_sub - 1 else False

        def up(f, h=h):
            cols = slice(f * ft, (f + 1) * ft)
            return _dot(h, wu_ref[0, :, cols]), _dot(h[top:bot], wu_ref[1, :, cols])

        pending = [up(f) for f in range(min(FFN_AHEAD, n_f))]
        acc = None
        for f in range(n_f):
            if f + FFN_AHEAD < n_f:
                pending.append(up(f + FFN_AHEAD))
            u, gv = pending.pop(0)
            cols = slice(f * ft, (f + 1) * ft)
            u = jnp.concatenate([jnp.where(pad_top, 0.0, u[:top]), u[top:bot], jnp.where(pad_bot, 0.0, u[bot:])], axis=0)
            cw = cw_ref[:, cols]
            uc = (cw[0:1] * pltpu.roll(u, 1, axis=0)[top:bot] + cw[1:2] * u[top:bot]
                  + cw[2:3] * pltpu.roll(u, u.shape[0] - 1, axis=0)[top:bot] + cb_ref[:, cols])
            act = uc * _sigmoid(uc) * gv
            part = _dot(_bf(act), wd_ref[cols, :])
            acc = part if acc is None else acc + part
        rs = slice(s * sub, (s + 1) * sub)
        o_ref[0, rs] = x[rs] + mod[5:6] * _rms(acc, ng[1:2])


def _ffn(x, mod, ng, w_up, conv_w, conv_b, w_down):
    b, t, d = x.shape
    f_dim = w_down.shape[0]
    rows = min(FFN_ROWS, t)
    halo = rows // BF16_ROWS
    n_halo = t // BF16_ROWS

    def full(a):
        nd = a.ndim
        return pl.BlockSpec(a.shape, lambda bi, i: (0,) * nd)

    conv_b = conv_b.reshape(1, f_dim)
    row = pl.BlockSpec((1, rows, d), lambda bi, i: (bi, i, 0))
    return pl.pallas_call(
        _ffn_kernel,
        grid=(b, t // rows),
        in_specs=[
            row,
            pl.BlockSpec((1, BF16_ROWS, d), lambda bi, i: (bi, jnp.maximum(i * halo - 1, 0), 0)),
            pl.BlockSpec((1, BF16_ROWS, d), lambda bi, i: (bi, jnp.minimum((i + 1) * halo, n_halo - 1), 0)),
            pl.BlockSpec((1, N_MOD, d), lambda bi, i: (bi, 0, 0)),
            full(ng), full(w_up), full(conv_w), full(conv_b), full(w_down),
        ],
        out_specs=row,
        out_shape=jax.ShapeDtypeStruct((b, t, d), F32),
        compiler_params=_cparams(("arbitrary", "arbitrary")),
        name="ffn",
    )(x, x, x, mod, ng, w_up, conv_w, conv_b, w_down)


def _rope_tables(t):
    pos = jnp.arange(t)
    row = (pos // GRID_W).astype(F32)
    col = (pos % GRID_W).astype(F32)
    n_pair = AT_HEAD // 4
    inv = ROPE_THETA ** (-jnp.arange(n_pair, dtype=F32) / n_pair)
    ang = jnp.concatenate([row[:, None] * inv, col[:, None] * inv], axis=-1)
    cos = jnp.repeat(jnp.cos(ang), 2, axis=-1)
    sin = jnp.repeat(jnp.sin(ang), 2, axis=-1)
    sign = jnp.where(jnp.arange(AT_HEAD) % 2 == 0, -1.0, 1.0)
    return cos, sin * sign


def _trunk(x, mod_all, cos, sin, norm_g, rw, at, ffn):
    v_first = None
    for l in range(mod_all.shape[0]):
        mod = mod_all[l]
        ng = norm_g[l]
        j = l // 2
        if l % 2 == 0:
            proj, vres, ka, lnx, w_o = rw[j]
            if vres is not None:
                vres = (v_first,) + vres
            r, k, v, kk, g, bonus, lw0, lw1, as0, as1 = _rw_proj(x, mod, ng[0:1], *proj, vres)
            if j == 0:
                v_first = v
            y = _wkv(r, k, v, kk, lw0, as0, ka, reverse=False)
            x = _wkv(r, k, v, kk, lw1, as1, ka, reverse=True, finish=(y, bonus, g, x, mod, ng[1:2], lnx, w_o))
        else:
            w_qkv, qk_g, w_o = at[j]
            q, k, vt = _at_qkv(x, mod, ng[0:1], w_qkv, qk_g, cos, sin)
            x = _at_out(_attention(q, k, vt), x, mod, ng[1:2], w_o)
        x = _ffn(x, mod, ng[2:4], *ffn[l])
    return x


def kernel(x_prompt, x_sample, c_prompt, c_sample, ada_w, ada_b, norm_g, rw_mu, rw_rkv, rw_w0, rw_wA, rw_wB, rw_a0,
           rw_aA, rw_aB, rw_gA, rw_gB, rw_kk, rw_ka, rw_rk, rw_lnx, rw_o, rw_v0, rw_vA, rw_vB, at_qkv, at_qk_g,
           at_o, ffn_up, ffn_conv_w, ffn_conv_b, ffn_down):
    n_prompt = x_prompt.shape[0]
    d = x_prompt.shape[2]
    depth = ada_w.shape[0]
    f_dim = ffn_down.shape[1]
    row = lambda z: z.reshape(1, d)

    def both_dirs(bias, w_in, w_out):
        zero = jnp.zeros_like(w_out[0])
        w_bd = jnp.concatenate([jnp.concatenate([w_out[0], zero], axis=1),
                                jnp.concatenate([zero, w_out[1]], axis=1)], axis=0)
        return bias.reshape(1, 2 * d), _bf(jnp.concatenate([w_in[0], w_in[1]], axis=1)), _bf(w_bd)

    rw = []
    for j in range(rw_rkv.shape[0]):
        proj = (rw_mu[j], _bf(rw_rkv[j]), *both_dirs(rw_w0[j], rw_wA[j], rw_wB[j]),
                *both_dirs(rw_a0[j], rw_aA[j], rw_aB[j]), _bf(rw_gA[j]), _bf(rw_gB[j]), row(rw_kk[j]),
                row(rw_ka[j]), row(rw_rk[j]))
        vres = None if j == 0 else (row(rw_v0[j - 1]), _bf(rw_vA[j - 1]), _bf(rw_vB[j - 1]))
        rw.append((proj, vres, row(rw_ka[j]), rw_lnx[j], _bf(rw_o[j])))
    at = [(_bf(at_qkv[j]), at_qk_g[j], _bf(at_o[j])) for j in range(at_qkv.shape[0])]
    ffn = [(_bf(ffn_up[l]).reshape(d, 2, f_dim).transpose(1, 0, 2), ffn_conv_w[l], ffn_conv_b[l], _bf(ffn_down[l]))
           for l in range(depth)]

    c = jnp.concatenate([c_prompt, c_sample], axis=0)
    mod_all = _ada_mod(c, ada_w, ada_b).reshape(depth, c.shape[0], N_MOD, d)
    outs = []
    for x, mod in ((x_prompt, mod_all[:, :n_prompt]), (x_sample, mod_all[:, n_prompt:])):
        cos, sin = _rope_tables(x.shape[1])
        outs.append(_trunk(x, mod, cos, sin, norm_g, rw, at, ffn))
    return tuple(outs)
```

```python
import functools

import jax
import jax.numpy as jnp
from jax import lax
from jax.experimental import pallas as pl
from jax.experimental.pallas import tpu as pltpu

F32 = jnp.float32
BF16 = jnp.bfloat16

RW_HEAD = 64
AT_HEAD = 128
AT_KVH = 2
GRID_W = 64
ROPE_THETA = 10000.0
NORM_EPS = 1e-6
GN_EPS = 64e-5
N_MOD = 6

LOG2_E = 1.4426950408889634
DECAY_CAP = 0.6065306597126334
LANES = 128
BF16_ROWS = 16
HEAD_SHIFT = RW_HEAD.bit_length() - 1
WKV_CHUNK = 64
WKV_BLOCK = 256
ROW_TILE = 256
ATT_Q_TILE = 256
ATT_K_BLOCK = 256
ATT_AHEAD = 12
FFN_TILE = 256
FFN_ROWS = 512
FFN_SUB_ROWS = 256
FFN_AHEAD = 2
VMEM_LIMIT = 56 * 1024 * 1024


def _dot(a, b):
    return jnp.dot(a, b, preferred_element_type=F32)


def _dot_nt(a, b):
    return lax.dot_general(a, b, (((1,), (1,)), ((), ())), preferred_element_type=F32)


def _dot_tn(a, b):
    return lax.dot_general(a, b, (((0,), (0,)), ((), ())), preferred_element_type=F32)


def _bf(x):
    return x.astype(BF16)


def _sigmoid(x):
    return 1.0 / (1.0 + jnp.exp(-x))


def _norm_mod(x, gain, shift, scale):
    ms = jnp.mean(x * x, axis=-1, keepdims=True)
    return (x * lax.rsqrt(ms + NORM_EPS)) * (gain * (1.0 + scale)) + shift


def _rms(x, gain):
    ms = jnp.mean(x * x, axis=-1, keepdims=True)
    return x * lax.rsqrt(ms + NORM_EPS) * gain


def _head_ones():
    r = lax.broadcasted_iota(jnp.int32, (LANES, LANES), 0) >> HEAD_SHIFT
    c = lax.broadcasted_iota(jnp.int32, (LANES, LANES), 1) >> HEAD_SHIFT
    return jnp.where(r == c, 1.0, 0.0).astype(BF16)


def _head_sum(x, ones_bd):
    hi = _bf(x)
    lo = _bf(x - hi.astype(F32))
    out = []
    for p in range(x.shape[1] // LANES):
        sl = slice(p * LANES, (p + 1) * LANES)
        out.append(_dot(hi[:, sl], ones_bd) + _dot(lo[:, sl], ones_bd))
    return jnp.concatenate(out, axis=1)


def _shift_rows_down(x, first_row):
    y = pltpu.roll(x, 1, axis=0)
    row = lax.broadcasted_iota(jnp.int32, (8, x.shape[1]), 0)
    head = jnp.where(row == 0, first_row, y[:8])
    return jnp.concatenate([head, y[8:]], axis=0)


def _shift_rows_up(x, last_row):
    n = x.shape[0]
    y = pltpu.roll(x, n - 1, axis=0)
    row = lax.broadcasted_iota(jnp.int32, (8, x.shape[1]), 0)
    tail = jnp.where(row == 7, last_row, y[n - 8:])
    return jnp.concatenate([y[:n - 8], tail], axis=0)


def _cparams(sem):
    return pltpu.CompilerParams(dimension_semantics=sem, vmem_limit_bytes=VMEM_LIMIT)


def _mod_kernel(c_ref, w_ref, b_ref, o_ref):
    c = c_ref[...]
    sc = c * _sigmoid(c)
    o_ref[0] = _dot(_bf(sc), _bf(w_ref[0])) + b_ref[0]


def _ada_mod(c, ada_w, ada_b):
    depth, d, n = ada_w.shape
    b = c.shape[0]
    tn = n // 4
    return pl.pallas_call(
        _mod_kernel,
        grid=(depth, n // tn),
        in_specs=[
            pl.BlockSpec((b, d), lambda l, j: (0, 0)),
            pl.BlockSpec((1, d, tn), lambda l, j: (l, 0, j)),
            pl.BlockSpec((1, 1, tn), lambda l, j: (l, 0, j)),
        ],
        out_specs=pl.BlockSpec((1, b, tn), lambda l, j: (l, 0, j)),
        out_shape=jax.ShapeDtypeStruct((depth, b, n), F32),
        compiler_params=_cparams(("arbitrary", "arbitrary")),
        name="ada_mod",
    )(c, ada_w, ada_b.reshape(depth, 1, n))


def _rw_proj_kernel(has_vres, x_ref, xp_ref, xn_ref, mod_ref, ng_ref, mu_ref, wrkv_ref, w0_ref, wA_ref, wB_ref,
                    a0_ref, aA_ref, aB_ref, gA_ref, gB_ref, kk_ref, ka_ref, rk_ref, *rest):
    if has_vres:
        vf_ref, v0_ref, vA_ref, vB_ref = rest[:4]
        rest = rest[4:]
    r_o, k_o, v_o, kkn_o, g_o, bonus_o, lw0_o, lw1_o, as0_o, as1_o = rest
    i = pl.program_id(1)
    nt = pl.num_programs(1)
    mod = mod_ref[0]
    shift, scale = mod[0:1], mod[1:2]
    gain = ng_ref[...]
    h = _norm_mod(x_ref[0], gain, shift, scale)
    h_before = _norm_mod(xp_ref[0], gain, shift, scale)[7:8]
    h_after = _norm_mod(xn_ref[0], gain, shift, scale)[0:1]
    h_before = jnp.where(i == 0, 0.0, h_before)
    h_after = jnp.where(i == nt - 1, 0.0, h_after)
    xx = 0.5 * (_shift_rows_down(h, h_before) + _shift_rows_up(h, h_after)) - h
    mu = mu_ref[...]
    ones_bd = _head_ones()
    n_lora = w0_ref.shape[1] // 2

    xr, xw, xk, xv, xa, xg = (_bf(h + xx * mu[n:n + 1]) for n in range(6))
    r = _dot(xr, wrkv_ref[0])
    k = _dot(xk, wrkv_ref[1])
    v = _dot(xv, wrkv_ref[2])
    if has_vres:
        mix = _sigmoid(v0_ref[...] + _dot(_bf(_dot(xv, vA_ref[...])), vB_ref[...]))
        v = v + (vf_ref[0] - v) * mix
    r_o[0] = r
    k_o[0] = k
    v_o[0] = v
    g_o[0] = _dot(_bf(_sigmoid(_dot(xg, gA_ref[...]))), gB_ref[...])
    kk = k * kk_ref[...]
    kkn_o[0] = kk * lax.rsqrt(jnp.maximum(_head_sum(kk * kk, ones_bd), 1e-24))
    wl = w0_ref[...] + _dot(_bf(jnp.tanh(_dot(xw, wA_ref[...]))), wB_ref[...])
    lw = -DECAY_CAP * _sigmoid(wl)
    a_sig = _sigmoid(a0_ref[...] + _dot(_bf(_dot(xa, aA_ref[...])), aB_ref[...]))
    lw0_o[0], lw1_o[0] = lw[:, :n_lora], lw[:, n_lora:]
    as0, as1 = a_sig[:, :n_lora], a_sig[:, n_lora:]
    as0_o[0], as1_o[0] = as0, as1
    kd_sum = k * (2.0 + (as0 + as1 - 2.0) * ka_ref[...])
    bonus_o[0] = _head_sum(r * kd_sum * rk_ref[...], ones_bd) * v


def _rw_proj(x, mod, ng, mu, w_rkv, w0, wA, wB, a0, aA, aB, gA, gB, k_k, k_a, r_k, vres):
    b, t, d = x.shape
    tq = min(ROW_TILE, t)
    nt = t // tq
    has_vres = vres is not None
    row = pl.BlockSpec((1, tq, d), lambda bi, i: (bi, i, 0))

    def full(a):
        nd = a.ndim
        return pl.BlockSpec(a.shape, lambda bi, i: (0,) * nd)

    halo = tq // 8
    n8 = t // 8
    in_specs = [
        row,
        pl.BlockSpec((1, 8, d), lambda bi, i: (bi, jnp.maximum(i * halo - 1, 0), 0)),
        pl.BlockSpec((1, 8, d), lambda bi, i: (bi, jnp.minimum((i + 1) * halo, n8 - 1), 0)),
        pl.BlockSpec((1, N_MOD, d), lambda bi, i: (bi, 0, 0)),
    ]
    params = [ng, mu, w_rkv, w0, wA, wB, a0, aA, aB, gA, gB, k_k, k_a, r_k]
    args = [x, x, x, mod] + params
    in_specs += [full(a) for a in params]
    if has_vres:
        v_first, v0, vA, vB = vres
        args += [v_first, v0, vA, vB]
        in_specs += [row, full(v0), full(vA), full(vB)]
    n_out = 10
    return pl.pallas_call(
        functools.partial(_rw_proj_kernel, has_vres),
        grid=(b, nt),
        in_specs=in_specs,
        out_specs=[row] * n_out,
        out_shape=[jax.ShapeDtypeStruct((b, t, d), F32)] * n_out,
        compiler_params=_cparams(("arbitrary", "arbitrary")),
        name="rw_proj",
    )(*args)


def _wkv_kernel(reverse, finish, r_ref, k_ref, v_ref, kk_ref, lw_ref, as_ref, ka_ref, tri_ref, *rest):
    if finish:
        ya_ref, bonus_ref, g_ref, x_ref, mod_ref, ng_ref, lnx_ref, wo_ref, o_ref, s_ref = rest
    else:
        o_ref, s_ref = rest
    L = WKV_CHUNK
    rows, d = lw_ref.shape[1], lw_ref.shape[2]
    n_chunk = rows // L
    n_tile = d // LANES

    @pl.when(pl.program_id(1) == 0)
    def _():
        s_ref[...] = jnp.zeros_like(s_ref)

    lw = lw_ref[0]
    hi = _bf(lw)
    rem = lw - hi.astype(F32)
    mid = _bf(rem)
    lo = _bf(rem - mid.astype(F32))
    tri = tri_ref[...]
    cs = _dot(tri, hi) + _dot(tri, mid) + _dot(tri, lo)

    w_in = jnp.exp(cs)
    w_inv = jnp.exp(-cs)
    w_ex = w_in * jnp.exp(-lw)
    ends = [c * L if reverse else (c + 1) * L - 1 for c in range(n_chunk)]
    dec = [jnp.exp(cs[e:e + 1]) for e in ends]
    kk = kk_ref[0]
    a_sig = as_ref[0]
    v = v_ref[0]
    a_t = -(kk * w_ex)
    r_t = r_ref[0] * w_in
    b_t = kk * a_sig * w_inv
    k_t = k_ref[0] * (1.0 + (a_sig - 1.0) * ka_ref[...]) * w_inv

    lane = lax.broadcasted_iota(jnp.int32, (L, LANES), 1)
    row = lax.broadcasted_iota(jnp.int32, (L, LANES), 0)
    col = lane & (L - 1)
    head0 = lane < RW_HEAD
    if reverse:
        strict, incl = row < col, row <= col
    else:
        strict, incl = row > col, row >= col
    eye2 = jnp.where(row == col, 1.0, 0.0)
    r2 = lax.broadcasted_iota(jnp.int32, (LANES, LANES), 0) >> HEAD_SHIFT
    c2 = lax.broadcasted_iota(jnp.int32, (LANES, LANES), 1) >> HEAD_SHIFT
    same_head = r2 == c2

    def bdiag(z):
        return _bf(jnp.concatenate([jnp.where(head0, z, 0.0), jnp.where(head0, 0.0, z)], axis=0))

    chunks = list(range(n_chunk - 1, -1, -1) if reverse else range(n_chunk))
    pairs = [(c, p) for c in chunks for p in range(n_tile)]
    idx = range(len(pairs))

    def cut(z):
        return [z[c * L:(c + 1) * L, p * LANES:(p + 1) * LANES] for c, p in pairs]

    def dots(lhs, w):
        return [_dot(lhs[i], w[i]) for i in idx]

    at, rt, bt, kt, vv = (cut(z) for z in (a_t, r_t, b_t, k_t, v))
    dec_t = [dec[c][:, p * LANES:(p + 1) * LANES] for c, p in pairs]
    bh = [bt[i] * dec_t[i] for i in idx]
    kh = [kt[i] * dec_t[i] for i in idx]
    vbd = [bdiag(vv[i]) for i in idx]
    gram = dots([_bf(jnp.concatenate([at[i], rt[i]], axis=0)) for i in idx],
                [jnp.concatenate([bdiag(bt[i]).T, bdiag(kt[i]).T], axis=1) for i in idx])
    a_ab = [jnp.where(strict, gram[i][:L, :LANES], 0.0) for i in idx]
    a_ak = [jnp.where(strict, gram[i][:L, LANES:], 0.0) for i in idx]
    a_rb = [_bf(jnp.where(incl, gram[i][L:, :LANES], 0.0)) for i in idx]
    a_rk = [jnp.where(incl, gram[i][L:, LANES:], 0.0) for i in idx]
    av = dots([_bf(jnp.concatenate([a_ak[i], a_rk[i]], axis=0)) for i in idx], vbd)
    akv = [av[i][:L] for i in idx]
    y0 = [av[i][L:] for i in idx]
    x_inv = [eye2 + jnp.where((row >> 1) == (col >> 1), a_ab[i], 0.0) for i in idx]
    lev = 1
    while (1 << lev) < L:
        sel = ((row >> (lev + 1)) == (col >> (lev + 1))) & ((row >> lev) != (col >> lev))
        step = dots([_bf(x_inv[i]) for i in idx], [bdiag(jnp.where(sel, a_ab[i], 0.0)) for i in idx])
        grow = dots([_bf(step[i]) for i in idx], [bdiag(x_inv[i]) for i in idx])
        x_inv = [x_inv[i] + grow[i] for i in idx]
        lev += 1
    sol = dots([_bf(x_inv[i]) for i in idx],
               [jnp.concatenate([bdiag(at[i]), bdiag(akv[i])], axis=1) for i in idx])
    at2 = [sol[i][:, :LANES] for i in idx]
    u0 = [sol[i][:, LANES:] for i in idx]
    mix = dots(a_rb, [jnp.concatenate([bdiag(at2[i]), bdiag(u0[i])], axis=1) for i in idx])
    r2_ = [_bf(rt[i] + mix[i][:, :LANES]) for i in idx]
    y0 = [y0[i] + mix[i][:, LANES:] for i in idx]
    p_bd = [_bf(jnp.where(same_head, _dot_tn(_bf(at2[i]), _bf(bh[i])), 0.0)) for i in idx]
    q_bd = [jnp.where(same_head,
                      _dot_tn(_bf(jnp.concatenate([u0[i], vv[i]], axis=0)),
                              _bf(jnp.concatenate([bh[i], kh[i]], axis=0))), 0.0) for i in idx]
    state = [s_ref[p] for p in range(n_tile)]
    y = {}
    for i, (c, p) in enumerate(pairs):
        s0 = state[p]
        s0b = _bf(s0)
        y[c, p] = _dot(r2_[i], s0b.T) + y0[i]
        state[p] = s0 * dec_t[i] + _dot(s0b, p_bd[i]) + q_bd[i]
    for p in range(n_tile):
        s_ref[p] = state[p]
    y = jnp.concatenate([jnp.concatenate([y[c, p] for p in range(n_tile)], axis=1) for c in range(n_chunk)], axis=0)
    if finish:
        o_ref[0] = _rw_finish(y + ya_ref[0], bonus_ref[0], g_ref[0], x_ref[0], mod_ref[0][2:3], ng_ref[...],
                              lnx_ref[...], wo_ref[...])
    else:
        o_ref[0] = y


def _rw_finish(y, bonus, g, x, gate, ng, lnx, w_o):
    ones_bd = _head_ones()
    inv_n = 1.0 / RW_HEAD
    mean = _head_sum(y, ones_bd) * inv_n
    yc = y - mean
    var = _head_sum(yc * yc, ones_bd) * inv_n
    yn = yc * lax.rsqrt(var + GN_EPS) * lnx[0:1] + lnx[1:2]
    out = _dot(_bf((yn + bonus) * g), w_o)
    return x + gate * _rms(out, ng)


def _wkv_tri(rows, reverse):
    t = jnp.arange(rows)
    same = (t[:, None] // WKV_CHUNK) == (t[None, :] // WKV_CHUNK)
    tri = (t[:, None] <= t[None, :]) if reverse else (t[:, None] >= t[None, :])
    return (same & tri).astype(BF16)


def _wkv(r, k, v, kk, lw, a_sig, k_a, reverse, finish=None):
    b, t, d = r.shape
    rows = min(WKV_BLOCK, t)
    nb = t // rows
    if reverse:
        blk = pl.BlockSpec((1, rows, d), lambda bi, i: (bi, nb - 1 - i, 0))
    else:
        blk = pl.BlockSpec((1, rows, d), lambda bi, i: (bi, i, 0))

    def full(a):
        nd = a.ndim
        return pl.BlockSpec(a.shape, lambda bi, i: (0,) * nd)

    tri = _wkv_tri(rows, reverse)
    args = [r, k, v, kk, lw, a_sig, k_a, tri]
    in_specs = [blk] * 6 + [full(k_a), full(tri)]
    if finish is not None:
        y_other, bonus, g, x, mod, ng, lnx, w_o = finish
        args += [y_other, bonus, g, x, mod, ng, lnx, w_o]
        in_specs += [blk] * 4 + [pl.BlockSpec((1, N_MOD, d), lambda bi, i: (bi, 0, 0)), full(ng), full(lnx), full(w_o)]
    return pl.pallas_call(
        functools.partial(_wkv_kernel, reverse, finish is not None),
        grid=(b, nb),
        in_specs=in_specs,
        out_specs=blk,
        out_shape=jax.ShapeDtypeStruct((b, t, d), F32),
        scratch_shapes=[pltpu.VMEM((d // LANES, LANES, LANES), F32)],
        compiler_params=_cparams(("arbitrary", "arbitrary")),
        name="wkv_bwd" if reverse else "wkv_fwd",
    )(*args)


def _rope(x, cos, sin_signed):
    return x * cos + pltpu.roll(x, AT_HEAD // 2, axis=1) * sin_signed


def _at_qkv_kernel(x_ref, mod_ref, ng_ref, w_ref, qkg_ref, cos_ref, sin_ref, q_o, k_o, vt_o):
    mod = mod_ref[0]
    h = _norm_mod(x_ref[0], ng_ref[...], mod[0:1], mod[1:2])
    qkv = _dot(_bf(h), w_ref[...])
    nq = q_o.shape[2]
    nk = k_o.shape[2]
    cos, sin = cos_ref[...], sin_ref[...]
    qkg = qkg_ref[...]
    scale = AT_HEAD ** -0.5 * LOG2_E

    def head(z, gain, mul):
        ms = jnp.mean(z * z, axis=-1, keepdims=True)
        return _rope(z * lax.rsqrt(ms + NORM_EPS) * gain, cos, sin) * mul

    for n in range(nq // AT_HEAD):
        sl = slice(n * AT_HEAD, (n + 1) * AT_HEAD)
        q_o[0, :, sl] = _bf(head(qkv[:, sl], qkg[0:1], scale))
    for n in range(nk // AT_HEAD):
        sl = slice(n * AT_HEAD, (n + 1) * AT_HEAD)
        k_o[0, :, sl] = _bf(head(qkv[:, nq + n * AT_HEAD:nq + (n + 1) * AT_HEAD], qkg[1:2], 1.0))
    vt_o[0] = _bf(qkv[:, nq + nk:].T)


def _at_qkv(x, mod, ng, w_qkv, qk_g, cos, sin):
    b, t, d = x.shape
    tq = min(ROW_TILE, t)
    nk = AT_KVH * AT_HEAD
    row = lambda n: pl.BlockSpec((1, tq, n), lambda bi, i: (bi, i, 0))
    return pl.pallas_call(
        _at_qkv_kernel,
        grid=(b, t // tq),
        in_specs=[
            row(d),
            pl.BlockSpec((1, N_MOD, d), lambda bi, i: (bi, 0, 0)),
            pl.BlockSpec((1, d), lambda bi, i: (0, 0)),
            pl.BlockSpec(w_qkv.shape, lambda bi, i: (0, 0)),
            pl.BlockSpec((2, AT_HEAD), lambda bi, i: (0, 0)),
            pl.BlockSpec((tq, AT_HEAD), lambda bi, i: (i, 0)),
            pl.BlockSpec((tq, AT_HEAD), lambda bi, i: (i, 0)),
        ],
        out_specs=[row(d), row(nk), pl.BlockSpec((1, nk, tq), lambda bi, i: (bi, 0, i))],
        out_shape=[jax.ShapeDtypeStruct((b, t, d), BF16), jax.ShapeDtypeStruct((b, t, nk), BF16),
                   jax.ShapeDtypeStruct((b, nk, t), BF16)],
        compiler_params=_cparams(("arbitrary", "arbitrary")),
        name="at_qkv",
    )(x, mod, ng, w_qkv, qk_g, cos, sin)


def _attn_kernel(q_ref, k_ref, vt_ref, x_ref, mod_ref, ng_ref, wo_ref, o_ref):
    n_head = q_ref.shape[2] // AT_HEAD
    group = n_head // AT_KVH
    t = k_ref.shape[1]
    kb = min(ATT_K_BLOCK, t)
    n_kb = t // kb
    ones = jnp.ones((BF16_ROWS, t), BF16)
    vt = [jnp.concatenate([vt_ref[0, h * AT_HEAD:(h + 1) * AT_HEAD], ones], axis=0) for h in range(AT_KVH)]
    blocks = [(n, b) for n in range(n_head) for b in range(n_kb)]

    def scores(n, b):
        h = n // group
        return _dot_nt(k_ref[0, b * kb:(b + 1) * kb, h * AT_HEAD:(h + 1) * AT_HEAD],
                       q_ref[0, :, n * AT_HEAD:(n + 1) * AT_HEAD])

    pending = [scores(*blk) for blk in blocks[:ATT_AHEAD]]
    m = acc = None
    heads = []
    for i, (n, b) in enumerate(blocks):
        if i + ATT_AHEAD < len(blocks):
            pending.append(scores(*blocks[i + ATT_AHEAD]))
        st = pending.pop(0)
        m_blk = jnp.max(st, axis=0, keepdims=True)
        if b == 0:
            m = m_blk
        else:
            m_new = jnp.maximum(m, m_blk)
            acc = acc * jnp.exp2(m - m_new)
            m = m_new
        part = _dot(vt[n // group][:, b * kb:(b + 1) * kb], jnp.exp2(_bf(st - m)))
        acc = part if b == 0 else acc + part
        if b == n_kb - 1:
            heads.append(_bf((acc[:AT_HEAD] / acc[AT_HEAD:AT_HEAD + 1]).T))
    out = _dot(jnp.concatenate(heads, axis=1), wo_ref[...])
    o_ref[0] = x_ref[0] + mod_ref[0][2:3] * _rms(out, ng_ref[...])


def _attention(q, k, vt, x, mod, ng, w_o):
    b, t, d = q.shape
    tq = min(ATT_Q_TILE, t)
    nk = k.shape[2]
    row = pl.BlockSpec((1, tq, d), lambda bi, i: (bi, i, 0))
    return pl.pallas_call(
        _attn_kernel,
        grid=(b, t // tq),
        in_specs=[row,
                  pl.BlockSpec((1, t, nk), lambda bi, i: (bi, 0, 0)),
                  pl.BlockSpec((1, nk, t), lambda bi, i: (bi, 0, 0)),
                  row,
                  pl.BlockSpec((1, N_MOD, d), lambda bi, i: (bi, 0, 0)),
                  pl.BlockSpec((1, d), lambda bi, i: (0, 0)),
                  pl.BlockSpec((d, d), lambda bi, i: (0, 0))],
        out_specs=row,
        out_shape=jax.ShapeDtypeStruct((b, t, d), F32),
        compiler_params=_cparams(("arbitrary", "arbitrary")),
        name="attention",
    )(q, k, vt, x, mod, ng, w_o)


def _ffn_kernel(x_ref, xp_ref, xn_ref, mod_ref, ng_ref, wu_ref, cw_ref, cb_ref, wd_ref, o_ref):
    i = pl.program_id(1)
    nt = pl.num_programs(1)
    rows = x_ref.shape[1]
    f_dim = wd_ref.shape[0]
    ft = min(FFN_TILE, f_dim)
    mod = mod_ref[0]
    ng = ng_ref[...]
    x = x_ref[0]
    h_all = _bf(_norm_mod(jnp.concatenate([xp_ref[0], x, xn_ref[0]], axis=0), ng[0:1], mod[3:4], mod[4:5]))
    n_f = f_dim // ft
    sub = min(FFN_SUB_ROWS, rows)
    n_sub = rows // sub
    top, bot = BF16_ROWS, BF16_ROWS + sub

    for s in range(n_sub):
        h = h_all[s * sub:s * sub + sub + 2 * BF16_ROWS]
        pad_top = (i == 0) if s == 0 else False
        pad_bot = (i == nt - 1) if s == n_sub - 1 else False

        def up(f, h=h):
            cols = slice(f * ft, (f + 1) * ft)
            return _dot(h, wu_ref[0, :, cols]), _dot(h[top:bot], wu_ref[1, :, cols])

        pending = [up(f) for f in range(min(FFN_AHEAD, n_f))]
        acc = None
        for f in range(n_f):
            if f + FFN_AHEAD < n_f:
                pending.append(up(f + FFN_AHEAD))
            u, gv = pending.pop(0)
            cols = slice(f * ft, (f + 1) * ft)
            u = jnp.concatenate([jnp.where(pad_top, 0.0, u[:top]), u[top:bot], jnp.where(pad_bot, 0.0, u[bot:])], axis=0)
            cw = cw_ref[:, cols]
            uc = (cw[0:1] * pltpu.roll(u, 1, axis=0)[top:bot] + cw[1:2] * u[top:bot]
                  + cw[2:3] * pltpu.roll(u, u.shape[0] - 1, axis=0)[top:bot] + cb_ref[:, cols])
            act = uc * _sigmoid(uc) * gv
            part = _dot(_bf(act), wd_ref[cols, :])
            acc = part if acc is None else acc + part
        rs = slice(s * sub, (s + 1) * sub)
        o_ref[0, rs] = x[rs] + mod[5:6] * _rms(acc, ng[1:2])


def _ffn(x, mod, ng, w_up, conv_w, conv_b, w_down):
    b, t, d = x.shape
    f_dim = w_down.shape[0]
    rows = min(FFN_ROWS, t)
    halo = rows // BF16_ROWS
    n_halo = t // BF16_ROWS

    def full(a):
        nd = a.ndim
        return pl.BlockSpec(a.shape, lambda bi, i: (0,) * nd)

    conv_b = conv_b.reshape(1, f_dim)
    row = pl.BlockSpec((1, rows, d), lambda bi, i: (bi, i, 0))
    return pl.pallas_call(
        _ffn_kernel,
        grid=(b, t // rows),
        in_specs=[
            row,
            pl.BlockSpec((1, BF16_ROWS, d), lambda bi, i: (bi, jnp.maximum(i * halo - 1, 0), 0)),
            pl.BlockSpec((1, BF16_ROWS, d), lambda bi, i: (bi, jnp.minimum((i + 1) * halo, n_halo - 1), 0)),
            pl.BlockSpec((1, N_MOD, d), lambda bi, i: (bi, 0, 0)),
            full(ng), full(w_up), full(conv_w), full(conv_b), full(w_down),
        ],
        out_specs=row,
        out_shape=jax.ShapeDtypeStruct((b, t, d), F32),
        compiler_params=_cparams(("arbitrary", "arbitrary")),
        name="ffn",
    )(x, x, x, mod, ng, w_up, conv_w, conv_b, w_down)


def _rope_tables(t):
    pos = jnp.arange(t)
    row = (pos // GRID_W).astype(F32)
    col = (pos % GRID_W).astype(F32)
    n_pair = AT_HEAD // 4
    inv = ROPE_THETA ** (-jnp.arange(n_pair, dtype=F32) / n_pair)
    ang = jnp.concatenate([row[:, None] * inv, col[:, None] * inv], axis=-1)
    cos, sin = jnp.cos(ang), jnp.sin(ang)
    return jnp.concatenate([cos, cos], axis=-1), jnp.concatenate([-sin, sin], axis=-1)


def _deinterleave_heads(w_qkv, qk_g, d):
    half = jnp.arange(AT_HEAD // 2)
    perm = jnp.concatenate([2 * half, 2 * half + 1])
    n_rot = d + AT_KVH * AT_HEAD
    cols = (jnp.arange(n_rot // AT_HEAD)[:, None] * AT_HEAD + perm[None, :]).reshape(-1)
    cols = jnp.concatenate([cols, jnp.arange(n_rot, w_qkv.shape[1])])
    return w_qkv[:, cols], qk_g[:, perm]


def _trunk(x, mod_all, cos, sin, norm_g, rw, at, ffn):
    v_first = None
    for l in range(mod_all.shape[0]):
        mod = mod_all[l]
        ng = norm_g[l]
        j = l // 2
        if l % 2 == 0:
            proj, vres, ka, lnx, w_o = rw[j]
            if vres is not None:
                vres = (v_first,) + vres
            r, k, v, kk, g, bonus, lw0, lw1, as0, as1 = _rw_proj(x, mod, ng[0:1], *proj, vres)
            if j == 0:
                v_first = v
            y = _wkv(r, k, v, kk, lw0, as0, ka, reverse=False)
            x = _wkv(r, k, v, kk, lw1, as1, ka, reverse=True, finish=(y, bonus, g, x, mod, ng[1:2], lnx, w_o))
        else:
            w_qkv, qk_g, w_o = at[j]
            q, k, vt = _at_qkv(x, mod, ng[0:1], w_qkv, qk_g, cos, sin)
            x = _attention(q, k, vt, x, mod, ng[1:2], w_o)
        x = _ffn(x, mod, ng[2:4], *ffn[l])
    return x


def kernel(x_prompt, x_sample, c_prompt, c_sample, ada_w, ada_b, norm_g, rw_mu, rw_rkv, rw_w0, rw_wA, rw_wB, rw_a0,
           rw_aA, rw_aB, rw_gA, rw_gB, rw_kk, rw_ka, rw_rk, rw_lnx, rw_o, rw_v0, rw_vA, rw_vB, at_qkv, at_qk_g,
           at_o, ffn_up, ffn_conv_w, ffn_conv_b, ffn_down):
    n_prompt = x_prompt.shape[0]
    d = x_prompt.shape[2]
    depth = ada_w.shape[0]
    f_dim = ffn_down.shape[1]
    row = lambda z: z.reshape(1, d)

    def both_dirs(bias, w_in, w_out):
        zero = jnp.zeros_like(w_out[0])
        w_bd = jnp.concatenate([jnp.concatenate([w_out[0], zero], axis=1),
                                jnp.concatenate([zero, w_out[1]], axis=1)], axis=0)
        return bias.reshape(1, 2 * d), _bf(jnp.concatenate([w_in[0], w_in[1]], axis=1)), _bf(w_bd)

    rw = []
    for j in range(rw_rkv.shape[0]):
        proj = (rw_mu[j], _bf(rw_rkv[j]), *both_dirs(rw_w0[j], rw_wA[j], rw_wB[j]),
                *both_dirs(rw_a0[j], rw_aA[j], rw_aB[j]), _bf(rw_gA[j]), _bf(rw_gB[j]), row(rw_kk[j]),
                row(rw_ka[j]), row(rw_rk[j]))
        vres = None if j == 0 else (row(rw_v0[j - 1]), _bf(rw_vA[j - 1]), _bf(rw_vB[j - 1]))
        rw.append((proj, vres, row(rw_ka[j]), rw_lnx[j], _bf(rw_o[j])))
    at = []
    for j in range(at_qkv.shape[0]):
        w_qkv, qk_g = _deinterleave_heads(at_qkv[j], at_qk_g[j], d)
        at.append((_bf(w_qkv), qk_g, _bf(at_o[j])))
    ffn = [(_bf(ffn_up[l]).reshape(d, 2, f_dim).transpose(1, 0, 2), ffn_conv_w[l], ffn_conv_b[l], _bf(ffn_down[l]))
           for l in range(depth)]

    c = jnp.concatenate([c_prompt, c_sample], axis=0)
    mod_all = _ada_mod(c, ada_w, ada_b).reshape(depth, c.shape[0], N_MOD, d)
    outs = []
    for x, mod in ((x_prompt, mod_all[:, :n_prompt]), (x_sample, mod_all[:, n_prompt:])):
        cos, sin = _rope_tables(x.shape[1])
        outs.append(_trunk(x, mod, cos, sin, norm_g, rw, at, ffn))
    return tuple(outs)
```

```python
import functools

import jax
import jax.numpy as jnp
from jax import lax
from jax.experimental import pallas as pl
from jax.experimental.pallas import tpu as pltpu

F32 = jnp.float32
BF16 = jnp.bfloat16

RW_HEAD = 64
AT_HEAD = 128
AT_KVH = 2
GRID_W = 64
ROPE_THETA = 10000.0
NORM_EPS = 1e-6
GN_EPS = 64e-5
N_MOD = 6

LOG2_E = 1.4426950408889634
DECAY_CAP = 0.6065306597126334
LANES = 128
BF16_ROWS = 16
HEAD_SHIFT = RW_HEAD.bit_length() - 1
WKV_CHUNK = 64
WKV_BLOCK = 256
WKV_HALF_K_FROM = 32
ROW_TILE = 256
ATT_Q_TILE = 256
ATT_K_BLOCK = 256
ATT_AHEAD = 12
FFN_TILE = 256
FFN_ROWS = 512
FFN_SUB_ROWS = 256
FFN_AHEAD = 2
VMEM_LIMIT = 56 * 1024 * 1024


def _dot(a, b):
    return jnp.dot(a, b, preferred_element_type=F32)


def _dot_nt(a, b):
    return lax.dot_general(a, b, (((1,), (1,)), ((), ())), preferred_element_type=F32)


def _dot_tn(a, b):
    return lax.dot_general(a, b, (((0,), (0,)), ((), ())), preferred_element_type=F32)


def _bf(x):
    return x.astype(BF16)


def _sigmoid(x):
    return 1.0 / (1.0 + jnp.exp(-x))


def _norm_mod(x, gain, shift, scale):
    ms = jnp.mean(x * x, axis=-1, keepdims=True)
    return (x * lax.rsqrt(ms + NORM_EPS)) * (gain * (1.0 + scale)) + shift


def _rms(x, gain):
    ms = jnp.mean(x * x, axis=-1, keepdims=True)
    return x * lax.rsqrt(ms + NORM_EPS) * gain


def _head_ones():
    r = lax.broadcasted_iota(jnp.int32, (LANES, LANES), 0) >> HEAD_SHIFT
    c = lax.broadcasted_iota(jnp.int32, (LANES, LANES), 1) >> HEAD_SHIFT
    return jnp.where(r == c, 1.0, 0.0).astype(BF16)


def _head_sum(x, ones_bd):
    hi = _bf(x)
    lo = _bf(x - hi.astype(F32))
    out = []
    for p in range(x.shape[1] // LANES):
        sl = slice(p * LANES, (p + 1) * LANES)
        out.append(_dot(hi[:, sl], ones_bd) + _dot(lo[:, sl], ones_bd))
    return jnp.concatenate(out, axis=1)


def _shift_rows_down(x, first_row):
    y = pltpu.roll(x, 1, axis=0)
    row = lax.broadcasted_iota(jnp.int32, (8, x.shape[1]), 0)
    head = jnp.where(row == 0, first_row, y[:8])
    return jnp.concatenate([head, y[8:]], axis=0)


def _shift_rows_up(x, last_row):
    n = x.shape[0]
    y = pltpu.roll(x, n - 1, axis=0)
    row = lax.broadcasted_iota(jnp.int32, (8, x.shape[1]), 0)
    tail = jnp.where(row == 7, last_row, y[n - 8:])
    return jnp.concatenate([y[:n - 8], tail], axis=0)


def _cparams(sem):
    return pltpu.CompilerParams(dimension_semantics=sem, vmem_limit_bytes=VMEM_LIMIT)


def _mod_kernel(c_ref, w_ref, b_ref, o_ref):
    c = c_ref[...]
    sc = c * _sigmoid(c)
    o_ref[0] = _dot(_bf(sc), _bf(w_ref[0])) + b_ref[0]


def _ada_mod(c, ada_w, ada_b):
    depth, d, n = ada_w.shape
    b = c.shape[0]
    tn = n // 4
    return pl.pallas_call(
        _mod_kernel,
        grid=(depth, n // tn),
        in_specs=[
            pl.BlockSpec((b, d), lambda l, j: (0, 0)),
            pl.BlockSpec((1, d, tn), lambda l, j: (l, 0, j)),
            pl.BlockSpec((1, 1, tn), lambda l, j: (l, 0, j)),
        ],
        out_specs=pl.BlockSpec((1, b, tn), lambda l, j: (l, 0, j)),
        out_shape=jax.ShapeDtypeStruct((depth, b, n), F32),
        compiler_params=_cparams(("arbitrary", "arbitrary")),
        name="ada_mod",
    )(c, ada_w, ada_b.reshape(depth, 1, n))


def _rw_proj_kernel(has_vres, x_ref, xp_ref, xn_ref, mod_ref, ng_ref, mu_ref, wrkv_ref, w0_ref, wA_ref, wB_ref,
                    a0_ref, aA_ref, aB_ref, gA_ref, gB_ref, kk_ref, ka_ref, rk_ref, *rest):
    if has_vres:
        vf_ref, v0_ref, vA_ref, vB_ref = rest[:4]
        rest = rest[4:]
    r_o, k_o, v_o, kkn_o, g_o, bonus_o, lw0_o, lw1_o, as0_o, as1_o = rest
    i = pl.program_id(1)
    nt = pl.num_programs(1)
    mod = mod_ref[0]
    shift, scale = mod[0:1], mod[1:2]
    gain = ng_ref[...]
    h = _norm_mod(x_ref[0], gain, shift, scale)
    h_before = _norm_mod(xp_ref[0], gain, shift, scale)[7:8]
    h_after = _norm_mod(xn_ref[0], gain, shift, scale)[0:1]
    h_before = jnp.where(i == 0, 0.0, h_before)
    h_after = jnp.where(i == nt - 1, 0.0, h_after)
    xx = 0.5 * (_shift_rows_down(h, h_before) + _shift_rows_up(h, h_after)) - h
    mu = mu_ref[...]
    ones_bd = _head_ones()
    n_lora = w0_ref.shape[1] // 2

    xr, xw, xk, xv, xa, xg = (_bf(h + xx * mu[n:n + 1]) for n in range(6))
    r = _dot(xr, wrkv_ref[0])
    k = _dot(xk, wrkv_ref[1])
    v = _dot(xv, wrkv_ref[2])
    if has_vres:
        mix = _sigmoid(v0_ref[...] + _dot(_bf(_dot(xv, vA_ref[...])), vB_ref[...]))
        v = v + (vf_ref[0] - v) * mix
    r_o[0] = r
    k_o[0] = k
    v_o[0] = v
    g_o[0] = _dot(_bf(_sigmoid(_dot(xg, gA_ref[...]))), gB_ref[...])
    kk = k * kk_ref[...]
    kkn_o[0] = kk * lax.rsqrt(jnp.maximum(_head_sum(kk * kk, ones_bd), 1e-24))
    wl = w0_ref[...] + _dot(_bf(jnp.tanh(_dot(xw, wA_ref[...]))), wB_ref[...])
    lw = -DECAY_CAP * _sigmoid(wl)
    a_sig = _sigmoid(a0_ref[...] + _dot(_bf(_dot(xa, aA_ref[...])), aB_ref[...]))
    lw0_o[0], lw1_o[0] = lw[:, :n_lora], lw[:, n_lora:]
    as0, as1 = a_sig[:, :n_lora], a_sig[:, n_lora:]
    as0_o[0], as1_o[0] = as0, as1
    kd_sum = k * (2.0 + (as0 + as1 - 2.0) * ka_ref[...])
    bonus_o[0] = _head_sum(r * kd_sum * rk_ref[...], ones_bd) * v


def _rw_proj(x, mod, ng, mu, w_rkv, w0, wA, wB, a0, aA, aB, gA, gB, k_k, k_a, r_k, vres):
    b, t, d = x.shape
    tq = min(ROW_TILE, t)
    nt = t // tq
    has_vres = vres is not None
    row = pl.BlockSpec((1, tq, d), lambda bi, i: (bi, i, 0))

    def full(a):
        nd = a.ndim
        return pl.BlockSpec(a.shape, lambda bi, i: (0,) * nd)

    halo = tq // 8
    n8 = t // 8
    in_specs = [
        row,
        pl.BlockSpec((1, 8, d), lambda bi, i: (bi, jnp.maximum(i * halo - 1, 0), 0)),
        pl.BlockSpec((1, 8, d), lambda bi, i: (bi, jnp.minimum((i + 1) * halo, n8 - 1), 0)),
        pl.BlockSpec((1, N_MOD, d), lambda bi, i: (bi, 0, 0)),
    ]
    params = [ng, mu, w_rkv, w0, wA, wB, a0, aA, aB, gA, gB, k_k, k_a, r_k]
    args = [x, x, x, mod] + params
    in_specs += [full(a) for a in params]
    if has_vres:
        v_first, v0, vA, vB = vres
        args += [v_first, v0, vA, vB]
        in_specs += [row, full(v0), full(vA), full(vB)]
    n_out = 10
    return pl.pallas_call(
        functools.partial(_rw_proj_kernel, has_vres),
        grid=(b, nt),
        in_specs=in_specs,
        out_specs=[row] * n_out,
        out_shape=[jax.ShapeDtypeStruct((b, t, d), F32)] * n_out,
        compiler_params=_cparams(("arbitrary", "arbitrary")),
        name="rw_proj",
    )(*args)


def _wkv_kernel(reverse, finish, r_ref, k_ref, v_ref, kk_ref, lw_ref, as_ref, ka_ref, tri_ref, *rest):
    if finish:
        ya_ref, bonus_ref, g_ref, x_ref, mod_ref, ng_ref, lnx_ref, wo_ref, o_ref, s_ref = rest
    else:
        o_ref, s_ref = rest
    L = WKV_CHUNK
    rows, d = lw_ref.shape[1], lw_ref.shape[2]
    n_chunk = rows // L
    n_tile = d // LANES

    @pl.when(pl.program_id(1) == 0)
    def _():
        s_ref[...] = jnp.zeros_like(s_ref)

    lw = lw_ref[0]
    hi = _bf(lw)
    rem = lw - hi.astype(F32)
    mid = _bf(rem)
    lo = _bf(rem - mid.astype(F32))
    tri = tri_ref[...]
    cs = _dot(tri, hi) + _dot(tri, mid) + _dot(tri, lo)

    w_in = jnp.exp(cs)
    w_inv = jnp.exp(-cs)
    w_ex = w_in * jnp.exp(-lw)
    ends = [c * L if reverse else (c + 1) * L - 1 for c in range(n_chunk)]
    dec = [jnp.exp(cs[e:e + 1]) for e in ends]
    kk = kk_ref[0]
    a_sig = as_ref[0]
    v = v_ref[0]
    a_t = -(kk * w_ex)
    r_t = r_ref[0] * w_in
    b_t = kk * a_sig * w_inv
    k_t = k_ref[0] * (1.0 + (a_sig - 1.0) * ka_ref[...]) * w_inv

    lane = lax.broadcasted_iota(jnp.int32, (L, LANES), 1)
    row = lax.broadcasted_iota(jnp.int32, (L, LANES), 0)
    col = lane & (L - 1)
    head0 = lane < RW_HEAD
    if reverse:
        strict, incl = row < col, row <= col
    else:
        strict, incl = row > col, row >= col
    eye2 = jnp.where(row == col, 1.0, 0.0)
    r2 = lax.broadcasted_iota(jnp.int32, (LANES, LANES), 0) >> HEAD_SHIFT
    c2 = lax.broadcasted_iota(jnp.int32, (LANES, LANES), 1) >> HEAD_SHIFT
    same_head = r2 == c2

    def bdiag(z):
        return _bf(jnp.concatenate([jnp.where(head0, z, 0.0), jnp.where(head0, 0.0, z)], axis=0))

    chunks = list(range(n_chunk - 1, -1, -1) if reverse else range(n_chunk))
    pairs = [(c, p) for c in chunks for p in range(n_tile)]
    idx = range(len(pairs))

    def cut(z):
        return [z[c * L:(c + 1) * L, p * LANES:(p + 1) * LANES] for c, p in pairs]

    def dots(lhs, w):
        return [_dot(lhs[i], w[i]) for i in idx]

    at, rt, bt, kt, vv = (cut(z) for z in (a_t, r_t, b_t, k_t, v))
    dec_t = [dec[c][:, p * LANES:(p + 1) * LANES] for c, p in pairs]
    bh = [bt[i] * dec_t[i] for i in idx]
    kh = [kt[i] * dec_t[i] for i in idx]
    vbd = [bdiag(vv[i]) for i in idx]
    gram = dots([_bf(jnp.concatenate([at[i], rt[i]], axis=0)) for i in idx],
                [jnp.concatenate([bdiag(bt[i]).T, bdiag(kt[i]).T], axis=1) for i in idx])
    a_ab = [jnp.where(strict, gram[i][:L, :LANES], 0.0) for i in idx]
    a_ak = [jnp.where(strict, gram[i][:L, LANES:], 0.0) for i in idx]
    a_rb = [_bf(jnp.where(incl, gram[i][L:, :LANES], 0.0)) for i in idx]
    a_rk = [jnp.where(incl, gram[i][L:, LANES:], 0.0) for i in idx]
    av = dots([_bf(jnp.concatenate([a_ak[i], a_rk[i]], axis=0)) for i in idx], vbd)
    akv = [av[i][:L] for i in idx]
    y0 = [av[i][L:] for i in idx]
    x_inv = [eye2 + jnp.where((row >> 1) == (col >> 1), a_ab[i], 0.0) for i in idx]
    lev = 1
    while (1 << lev) < L:
        sel = ((row >> (lev + 1)) == (col >> (lev + 1))) & ((row >> lev) != (col >> lev))
        n = 1 << lev
        if n >= WKV_HALF_K_FROM:
            late, early = (0, 1) if reverse else (1, 0)

            def half_cols(z, par):
                return jnp.concatenate([z[:, h * L + r * n:h * L + (r + 1) * n]
                                        for h in range(LANES // RW_HEAD) for r in range(L // n) if (r & 1) == par], axis=1)

            def half_rows(z, par):
                out = []
                for h in range(LANES // RW_HEAD):
                    zh = jnp.where(head0, z, 0.0) if h == 0 else jnp.where(head0, 0.0, z)
                    out += [zh[r * n:(r + 1) * n] for r in range(L // n) if (r & 1) == par]
                return _bf(jnp.concatenate(out, axis=0))

            step = dots([_bf(half_cols(x_inv[i], late)) for i in idx],
                        [half_rows(jnp.where(sel, a_ab[i], 0.0), late) for i in idx])
            grow = dots([_bf(half_cols(step[i], early)) for i in idx], [half_rows(x_inv[i], early) for i in idx])
        else:
            step = dots([_bf(x_inv[i]) for i in idx], [bdiag(jnp.where(sel, a_ab[i], 0.0)) for i in idx])
            grow = dots([_bf(step[i]) for i in idx], [bdiag(x_inv[i]) for i in idx])
        x_inv = [x_inv[i] + grow[i] for i in idx]
        lev += 1
    sol = dots([_bf(x_inv[i]) for i in idx],
               [jnp.concatenate([bdiag(at[i]), bdiag(akv[i])], axis=1) for i in idx])
    at2 = [sol[i][:, :LANES] for i in idx]
    u0 = [sol[i][:, LANES:] for i in idx]
    mix = dots(a_rb, [jnp.concatenate([bdiag(at2[i]), bdiag(u0[i])], axis=1) for i in idx])
    r2_ = [_bf(rt[i] + mix[i][:, :LANES]) for i in idx]
    y0 = [y0[i] + mix[i][:, LANES:] for i in idx]
    p_bd = [_bf(jnp.where(same_head, _dot_tn(_bf(at2[i]), _bf(bh[i])), 0.0)) for i in idx]
    q_bd = [jnp.where(same_head,
                      _dot_tn(_bf(jnp.concatenate([u0[i], vv[i]], axis=0)),
                              _bf(jnp.concatenate([bh[i], kh[i]], axis=0))), 0.0) for i in idx]
    state = [s_ref[p] for p in range(n_tile)]
    y = {}
    for i, (c, p) in enumerate(pairs):
        s0 = state[p]
        s0b = _bf(s0)
        y[c, p] = _dot(r2_[i], s0b.T) + y0[i]
        state[p] = s0 * dec_t[i] + _dot(s0b, p_bd[i]) + q_bd[i]
    for p in range(n_tile):
        s_ref[p] = state[p]
    y = jnp.concatenate([jnp.concatenate([y[c, p] for p in range(n_tile)], axis=1) for c in range(n_chunk)], axis=0)
    if finish:
        o_ref[0] = _rw_finish(y + ya_ref[0], bonus_ref[0], g_ref[0], x_ref[0], mod_ref[0][2:3], ng_ref[...],
                              lnx_ref[...], wo_ref[...])
    else:
        o_ref[0] = y


def _rw_finish(y, bonus, g, x, gate, ng, lnx, w_o):
    ones_bd = _head_ones()
    inv_n = 1.0 / RW_HEAD
    mean = _head_sum(y, ones_bd) * inv_n
    yc = y - mean
    var = _head_sum(yc * yc, ones_bd) * inv_n
    yn = yc * lax.rsqrt(var + GN_EPS) * lnx[0:1] + lnx[1:2]
    out = _dot(_bf((yn + bonus) * g), w_o)
    return x + gate * _rms(out, ng)


def _wkv_tri(rows, reverse):
    t = jnp.arange(rows)
    same = (t[:, None] // WKV_CHUNK) == (t[None, :] // WKV_CHUNK)
    tri = (t[:, None] <= t[None, :]) if reverse else (t[:, None] >= t[None, :])
    return (same & tri).astype(BF16)


def _wkv(r, k, v, kk, lw, a_sig, k_a, reverse, finish=None):
    b, t, d = r.shape
    rows = min(WKV_BLOCK, t)
    nb = t // rows
    if reverse:
        blk = pl.BlockSpec((1, rows, d), lambda bi, i: (bi, nb - 1 - i, 0))
    else:
        blk = pl.BlockSpec((1, rows, d), lambda bi, i: (bi, i, 0))

    def full(a):
        nd = a.ndim
        return pl.BlockSpec(a.shape, lambda bi, i: (0,) * nd)

    tri = _wkv_tri(rows, reverse)
    args = [r, k, v, kk, lw, a_sig, k_a, tri]
    in_specs = [blk] * 6 + [full(k_a), full(tri)]
    if finish is not None:
        y_other, bonus, g, x, mod, ng, lnx, w_o = finish
        args += [y_other, bonus, g, x, mod, ng, lnx, w_o]
        in_specs += [blk] * 4 + [pl.BlockSpec((1, N_MOD, d), lambda bi, i: (bi, 0, 0)), full(ng), full(lnx), full(w_o)]
    return pl.pallas_call(
        functools.partial(_wkv_kernel, reverse, finish is not None),
        grid=(b, nb),
        in_specs=in_specs,
        out_specs=blk,
        out_shape=jax.ShapeDtypeStruct((b, t, d), F32),
        scratch_shapes=[pltpu.VMEM((d // LANES, LANES, LANES), F32)],
        compiler_params=_cparams(("arbitrary", "arbitrary")),
        name="wkv_bwd" if reverse else "wkv_fwd",
    )(*args)


def _rope(x, cos, sin_signed):
    return x * cos + pltpu.roll(x, AT_HEAD // 2, axis=1) * sin_signed


def _at_qkv_kernel(x_ref, mod_ref, ng_ref, w_ref, qkg_ref, cos_ref, sin_ref, q_o, k_o, vt_o):
    mod = mod_ref[0]
    h = _norm_mod(x_ref[0], ng_ref[...], mod[0:1], mod[1:2])
    qkv = _dot(_bf(h), w_ref[...])
    nq = q_o.shape[2]
    nk = k_o.shape[2]
    cos, sin = cos_ref[...], sin_ref[...]
    qkg = qkg_ref[...]
    scale = AT_HEAD ** -0.5 * LOG2_E

    def head(z, gain, mul):
        ms = jnp.mean(z * z, axis=-1, keepdims=True)
        return _rope(z * lax.rsqrt(ms + NORM_EPS) * gain, cos, sin) * mul

    for n in range(nq // AT_HEAD):
        sl = slice(n * AT_HEAD, (n + 1) * AT_HEAD)
        q_o[0, :, sl] = _bf(head(qkv[:, sl], qkg[0:1], scale))
    for n in range(nk // AT_HEAD):
        sl = slice(n * AT_HEAD, (n + 1) * AT_HEAD)
        k_o[0, :, sl] = _bf(head(qkv[:, nq + n * AT_HEAD:nq + (n + 1) * AT_HEAD], qkg[1:2], 1.0))
    vt_o[0] = _bf(qkv[:, nq + nk:].T)


def _at_qkv(x, mod, ng, w_qkv, qk_g, cos, sin):
    b, t, d = x.shape
    tq = min(ROW_TILE, t)
    nk = AT_KVH * AT_HEAD
    row = lambda n: pl.BlockSpec((1, tq, n), lambda bi, i: (bi, i, 0))
    return pl.pallas_call(
        _at_qkv_kernel,
        grid=(b, t // tq),
        in_specs=[
            row(d),
            pl.BlockSpec((1, N_MOD, d), lambda bi, i: (bi, 0, 0)),
            pl.BlockSpec((1, d), lambda bi, i: (0, 0)),
            pl.BlockSpec(w_qkv.shape, lambda bi, i: (0, 0)),
            pl.BlockSpec((2, AT_HEAD), lambda bi, i: (0, 0)),
            pl.BlockSpec((tq, AT_HEAD), lambda bi, i: (i, 0)),
            pl.BlockSpec((tq, AT_HEAD), lambda bi, i: (i, 0)),
        ],
        out_specs=[row(d), row(nk), pl.BlockSpec((1, nk, tq), lambda bi, i: (bi, 0, i))],
        out_shape=[jax.ShapeDtypeStruct((b, t, d), BF16), jax.ShapeDtypeStruct((b, t, nk), BF16),
                   jax.ShapeDtypeStruct((b, nk, t), BF16)],
        compiler_params=_cparams(("arbitrary", "arbitrary")),
        name="at_qkv",
    )(x, mod, ng, w_qkv, qk_g, cos, sin)


def _attn_kernel(q_ref, k_ref, vt_ref, x_ref, mod_ref, ng_ref, wo_ref, o_ref):
    n_head = q_ref.shape[2] // AT_HEAD
    group = n_head // AT_KVH
    t = k_ref.shape[1]
    kb = min(ATT_K_BLOCK, t)
    n_kb = t // kb
    ones = jnp.ones((BF16_ROWS, t), BF16)
    vt = [jnp.concatenate([vt_ref[0, h * AT_HEAD:(h + 1) * AT_HEAD], ones], axis=0) for h in range(AT_KVH)]
    blocks = [(n, b) for n in range(n_head) for b in range(n_kb)]

    def scores(n, b):
        h = n // group
        return _dot_nt(k_ref[0, b * kb:(b + 1) * kb, h * AT_HEAD:(h + 1) * AT_HEAD],
                       q_ref[0, :, n * AT_HEAD:(n + 1) * AT_HEAD])

    pending = [scores(*blk) for blk in blocks[:ATT_AHEAD]]
    m = acc = None
    heads = []
    for i, (n, b) in enumerate(blocks):
        if i + ATT_AHEAD < len(blocks):
            pending.append(scores(*blocks[i + ATT_AHEAD]))
        st = pending.pop(0)
        m_blk = jnp.max(st, axis=0, keepdims=True)
        if b == 0:
            m = m_blk
        else:
            m_new = jnp.maximum(m, m_blk)
            acc = acc * jnp.exp2(m - m_new)
            m = m_new
        part = _dot(vt[n // group][:, b * kb:(b + 1) * kb], jnp.exp2(_bf(st - m)))
        acc = part if b == 0 else acc + part
        if b == n_kb - 1:
            heads.append(_bf((acc[:AT_HEAD] / acc[AT_HEAD:AT_HEAD + 1]).T))
    out = _dot(jnp.concatenate(heads, axis=1), wo_ref[...])
    o_ref[0] = x_ref[0] + mod_ref[0][2:3] * _rms(out, ng_ref[...])


def _attention(q, k, vt, x, mod, ng, w_o):
    b, t, d = q.shape
    tq = min(ATT_Q_TILE, t)
    nk = k.shape[2]
    row = pl.BlockSpec((1, tq, d), lambda bi, i: (bi, i, 0))
    return pl.pallas_call(
        _attn_kernel,
        grid=(b, t // tq),
        in_specs=[row,
                  pl.BlockSpec((1, t, nk), lambda bi, i: (bi, 0, 0)),
                  pl.BlockSpec((1, nk, t), lambda bi, i: (bi, 0, 0)),
                  row,
                  pl.BlockSpec((1, N_MOD, d), lambda bi, i: (bi, 0, 0)),
                  pl.BlockSpec((1, d), lambda bi, i: (0, 0)),
                  pl.BlockSpec((d, d), lambda bi, i: (0, 0))],
        out_specs=row,
        out_shape=jax.ShapeDtypeStruct((b, t, d), F32),
        compiler_params=_cparams(("arbitrary", "arbitrary")),
        name="attention",
    )(q, k, vt, x, mod, ng, w_o)


def _ffn_kernel(x_ref, xp_ref, xn_ref, mod_ref, ng_ref, wu_ref, cw_ref, cb_ref, wd_ref, o_ref):
    i = pl.program_id(1)
    nt = pl.num_programs(1)
    rows = x_ref.shape[1]
    f_dim = wd_ref.shape[0]
    ft = min(FFN_TILE, f_dim)
    mod = mod_ref[0]
    ng = ng_ref[...]
    x = x_ref[0]
    h_all = _bf(_norm_mod(jnp.concatenate([xp_ref[0], x, xn_ref[0]], axis=0), ng[0:1], mod[3:4], mod[4:5]))
    n_f = f_dim // ft
    sub = min(FFN_SUB_ROWS, rows)
    n_sub = rows // sub
    top, bot = BF16_ROWS, BF16_ROWS + sub

    for s in range(n_sub):
        h = h_all[s * sub:s * sub + sub + 2 * BF16_ROWS]
        pad_top = (i == 0) if s == 0 else False
        pad_bot = (i == nt - 1) if s == n_sub - 1 else False

        def up(f, h=h):
            cols = slice(f * ft, (f + 1) * ft)
            return _dot(h, wu_ref[0, :, cols]), _dot(h[top:bot], wu_ref[1, :, cols])

        pending = [up(f) for f in range(min(FFN_AHEAD, n_f))]
        acc = None
        for f in range(n_f):
            if f + FFN_AHEAD < n_f:
                pending.append(up(f + FFN_AHEAD))
            u, gv = pending.pop(0)
            cols = slice(f * ft, (f + 1) * ft)
            u = jnp.concatenate([jnp.where(pad_top, 0.0, u[:top]), u[top:bot], jnp.where(pad_bot, 0.0, u[bot:])], axis=0)
            cw = cw_ref[:, cols]
            uc = (cw[0:1] * pltpu.roll(u, 1, axis=0)[top:bot] + cw[1:2] * u[top:bot]
                  + cw[2:3] * pltpu.roll(u, u.shape[0] - 1, axis=0)[top:bot] + cb_ref[:, cols])
            act = uc * _sigmoid(uc) * gv
            part = _dot(_bf(act), wd_ref[cols, :])
            acc = part if acc is None else acc + part
        rs = slice(s * sub, (s + 1) * sub)
        o_ref[0, rs] = x[rs] + mod[5:6] * _rms(acc, ng[1:2])


def _ffn(x, mod, ng, w_up, conv_w, conv_b, w_down):
    b, t, d = x.shape
    f_dim = w_down.shape[0]
    rows = min(FFN_ROWS, t)
    halo = rows // BF16_ROWS
    n_halo = t // BF16_ROWS

    def full(a):
        nd = a.ndim
        return pl.BlockSpec(a.shape, lambda bi, i: (0,) * nd)

    conv_b = conv_b.reshape(1, f_dim)
    row = pl.BlockSpec((1, rows, d), lambda bi, i: (bi, i, 0))
    return pl.pallas_call(
        _ffn_kernel,
        grid=(b, t // rows),
        in_specs=[
            row,
            pl.BlockSpec((1, BF16_ROWS, d), lambda bi, i: (bi, jnp.maximum(i * halo - 1, 0), 0)),
            pl.BlockSpec((1, BF16_ROWS, d), lambda bi, i: (bi, jnp.minimum((i + 1) * halo, n_halo - 1), 0)),
            pl.BlockSpec((1, N_MOD, d), lambda bi, i: (bi, 0, 0)),
            full(ng), full(w_up), full(conv_w), full(conv_b), full(w_down),
        ],
        out_specs=row,
        out_shape=jax.ShapeDtypeStruct((b, t, d), F32),
        compiler_params=_cparams(("arbitrary", "arbitrary")),
        name="ffn",
    )(x, x, x, mod, ng, w_up, conv_w, conv_b, w_down)


def _rope_tables(t):
    pos = jnp.arange(t)
    row = (pos // GRID_W).astype(F32)
    col = (pos % GRID_W).astype(F32)
    n_pair = AT_HEAD // 4
    inv = ROPE_THETA ** (-jnp.arange(n_pair, dtype=F32) / n_pair)
    ang = jnp.concatenate([row[:, None] * inv, col[:, None] * inv], axis=-1)
    cos, sin = jnp.cos(ang), jnp.sin(ang)
    return jnp.concatenate([cos, cos], axis=-1), jnp.concatenate([-sin, sin], axis=-1)


def _deinterleave_heads(w_qkv, qk_g, d):
    half = jnp.arange(AT_HEAD // 2)
    perm = jnp.concatenate([2 * half, 2 * half + 1])
    n_rot = d + AT_KVH * AT_HEAD
    cols = (jnp.arange(n_rot // AT_HEAD)[:, None] * AT_HEAD + perm[None, :]).reshape(-1)
    cols = jnp.concatenate([cols, jnp.arange(n_rot, w_qkv.shape[1])])
    return w_qkv[:, cols], qk_g[:, perm]


def _trunk(x, mod_all, cos, sin, norm_g, rw, at, ffn):
    v_first = None
    for l in range(mod_all.shape[0]):
        mod = mod_all[l]
        ng = norm_g[l]
        j = l // 2
        if l % 2 == 0:
            proj, vres, ka, lnx, w_o = rw[j]
            if vres is not None:
                vres = (v_first,) + vres
            r, k, v, kk, g, bonus, lw0, lw1, as0, as1 = _rw_proj(x, mod, ng[0:1], *proj, vres)
            if j == 0:
                v_first = v
            y = _wkv(r, k, v, kk, lw0, as0, ka, reverse=False)
            x = _wkv(r, k, v, kk, lw1, as1, ka, reverse=True, finish=(y, bonus, g, x, mod, ng[1:2], lnx, w_o))
        else:
            w_qkv, qk_g, w_o = at[j]
            q, k, vt = _at_qkv(x, mod, ng[0:1], w_qkv, qk_g, cos, sin)
            x = _attention(q, k, vt, x, mod, ng[1:2], w_o)
        x = _ffn(x, mod, ng[2:4], *ffn[l])
    return x


def kernel(x_prompt, x_sample, c_prompt, c_sample, ada_w, ada_b, norm_g, rw_mu, rw_rkv, rw_w0, rw_wA, rw_wB, rw_a0,
           rw_aA, rw_aB, rw_gA, rw_gB, rw_kk, rw_ka, rw_rk, rw_lnx, rw_o, rw_v0, rw_vA, rw_vB, at_qkv, at_qk_g,
           at_o, ffn_up, ffn_conv_w, ffn_conv_b, ffn_down):
    n_prompt = x_prompt.shape[0]
    d = x_prompt.shape[2]
    depth = ada_w.shape[0]
    f_dim = ffn_down.shape[1]
    row = lambda z: z.reshape(1, d)

    def both_dirs(bias, w_in, w_out):
        zero = jnp.zeros_like(w_out[0])
        w_bd = jnp.concatenate([jnp.concatenate([w_out[0], zero], axis=1),
                                jnp.concatenate([zero, w_out[1]], axis=1)], axis=0)
        return bias.reshape(1, 2 * d), _bf(jnp.concatenate([w_in[0], w_in[1]], axis=1)), _bf(w_bd)

    rw = []
    for j in range(rw_rkv.shape[0]):
        proj = (rw_mu[j], _bf(rw_rkv[j]), *both_dirs(rw_w0[j], rw_wA[j], rw_wB[j]),
                *both_dirs(rw_a0[j], rw_aA[j], rw_aB[j]), _bf(rw_gA[j]), _bf(rw_gB[j]), row(rw_kk[j]),
                row(rw_ka[j]), row(rw_rk[j]))
        vres = None if j == 0 else (row(rw_v0[j - 1]), _bf(rw_vA[j - 1]), _bf(rw_vB[j - 1]))
        rw.append((proj, vres, row(rw_ka[j]), rw_lnx[j], _bf(rw_o[j])))
    at = []
    for j in range(at_qkv.shape[0]):
        w_qkv, qk_g = _deinterleave_heads(at_qkv[j], at_qk_g[j], d)
        at.append((_bf(w_qkv), qk_g, _bf(at_o[j])))
    ffn = [(_bf(ffn_up[l]).reshape(d, 2, f_dim).transpose(1, 0, 2), ffn_conv_w[l], ffn_conv_b[l], _bf(ffn_down[l]))
           for l in range(depth)]

    c = jnp.concatenate([c_prompt, c_sample], axis=0)
    mod_all = _ada_mod(c, ada_w, ada_b).reshape(depth, c.shape[0], N_MOD, d)
    outs = []
    for x, mod in ((x_prompt, mod_all[:, :n_prompt]), (x_sample, mod_all[:, n_prompt:])):
        cos, sin = _rope_tables(x.shape[1])
        outs.append(_trunk(x, mod, cos, sin, norm_g, rw, at, ffn))
    return tuple(outs)
```

```python
import functools

import jax
import jax.numpy as jnp
from jax import lax
from jax.experimental import pallas as pl
from jax.experimental.pallas import tpu as pltpu

F32 = jnp.float32
BF16 = jnp.bfloat16

RW_HEAD = 64
AT_HEAD = 128
AT_KVH = 2
GRID_W = 64
ROPE_THETA = 10000.0
NORM_EPS = 1e-6
GN_EPS = 64e-5
N_MOD = 6

LOG2_E = 1.4426950408889634
DECAY_CAP = 0.6065306597126334
LANES = 128
BF16_ROWS = 16
HEAD_SHIFT = RW_HEAD.bit_length() - 1
WKV_CHUNK = 64
WKV_BLOCK = 256
ROW_TILE = 256
ATT_Q_TILE = 256
ATT_K_BLOCK = 256
ATT_AHEAD = 12
FFN_TILE = 256
FFN_ROWS = 1024
FFN_SUB_ROWS = 256
FFN_AHEAD = 2
VMEM_LIMIT = 56 * 1024 * 1024


def _dot(a, b):
    return jnp.dot(a, b, preferred_element_type=F32)


def _dot_nt(a, b):
    return lax.dot_general(a, b, (((1,), (1,)), ((), ())), preferred_element_type=F32)


def _dot_tn(a, b):
    return lax.dot_general(a, b, (((0,), (0,)), ((), ())), preferred_element_type=F32)


def _bf(x):
    return x.astype(BF16)


def _sigmoid(x):
    return 1.0 / (1.0 + jnp.exp(-x))


def _norm_mod(x, gain, shift, scale):
    ms = jnp.mean(x * x, axis=-1, keepdims=True)
    return (x * lax.rsqrt(ms + NORM_EPS)) * (gain * (1.0 + scale)) + shift


def _rms(x, gain):
    ms = jnp.mean(x * x, axis=-1, keepdims=True)
    return x * lax.rsqrt(ms + NORM_EPS) * gain


def _head_ones():
    r = lax.broadcasted_iota(jnp.int32, (LANES, LANES), 0) >> HEAD_SHIFT
    c = lax.broadcasted_iota(jnp.int32, (LANES, LANES), 1) >> HEAD_SHIFT
    return jnp.where(r == c, 1.0, 0.0).astype(BF16)


def _head_sum(x, ones_bd):
    hi = _bf(x)
    lo = _bf(x - hi.astype(F32))
    out = []
    for p in range(x.shape[1] // LANES):
        sl = slice(p * LANES, (p + 1) * LANES)
        out.append(_dot(hi[:, sl], ones_bd) + _dot(lo[:, sl], ones_bd))
    return jnp.concatenate(out, axis=1)


def _shift_rows_down(x, first_row):
    y = pltpu.roll(x, 1, axis=0)
    row = lax.broadcasted_iota(jnp.int32, (8, x.shape[1]), 0)
    head = jnp.where(row == 0, first_row, y[:8])
    return jnp.concatenate([head, y[8:]], axis=0)


def _shift_rows_up(x, last_row):
    n = x.shape[0]
    y = pltpu.roll(x, n - 1, axis=0)
    row = lax.broadcasted_iota(jnp.int32, (8, x.shape[1]), 0)
    tail = jnp.where(row == 7, last_row, y[n - 8:])
    return jnp.concatenate([y[:n - 8], tail], axis=0)


def _cparams(sem):
    return pltpu.CompilerParams(dimension_semantics=sem, vmem_limit_bytes=VMEM_LIMIT)


def _mod_kernel(c_ref, w_ref, b_ref, o_ref):
    c = c_ref[...]
    sc = c * _sigmoid(c)
    o_ref[0] = _dot(_bf(sc), _bf(w_ref[0])) + b_ref[0]


def _ada_mod(c, ada_w, ada_b):
    depth, d, n = ada_w.shape
    b = c.shape[0]
    tn = n // 4
    return pl.pallas_call(
        _mod_kernel,
        grid=(depth, n // tn),
        in_specs=[
            pl.BlockSpec((b, d), lambda l, j: (0, 0)),
            pl.BlockSpec((1, d, tn), lambda l, j: (l, 0, j)),
            pl.BlockSpec((1, 1, tn), lambda l, j: (l, 0, j)),
        ],
        out_specs=pl.BlockSpec((1, b, tn), lambda l, j: (l, 0, j)),
        out_shape=jax.ShapeDtypeStruct((depth, b, n), F32),
        compiler_params=_cparams(("arbitrary", "arbitrary")),
        name="ada_mod",
    )(c, ada_w, ada_b.reshape(depth, 1, n))


def _rw_proj_kernel(has_vres, x_ref, xp_ref, xn_ref, mod_ref, ng_ref, mu_ref, wrkv_ref, w0_ref, wA_ref, wB_ref,
                    a0_ref, aA_ref, aB_ref, gA_ref, gB_ref, kk_ref, ka_ref, rk_ref, *rest):
    if has_vres:
        vf_ref, v0_ref, vA_ref, vB_ref = rest[:4]
        rest = rest[4:]
    r_o, k_o, v_o, kkn_o, g_o, bonus_o, lw0_o, lw1_o, as0_o, as1_o = rest
    i = pl.program_id(1)
    nt = pl.num_programs(1)
    mod = mod_ref[0]
    shift, scale = mod[0:1], mod[1:2]
    gain = ng_ref[...]
    h = _norm_mod(x_ref[0], gain, shift, scale)
    h_before = _norm_mod(xp_ref[0], gain, shift, scale)[7:8]
    h_after = _norm_mod(xn_ref[0], gain, shift, scale)[0:1]
    h_before = jnp.where(i == 0, 0.0, h_before)
    h_after = jnp.where(i == nt - 1, 0.0, h_after)
    xx = 0.5 * (_shift_rows_down(h, h_before) + _shift_rows_up(h, h_after)) - h
    mu = mu_ref[...]
    ones_bd = _head_ones()
    n_lora = w0_ref.shape[1] // 2

    xr, xw, xk, xv, xa, xg = (_bf(h + xx * mu[n:n + 1]) for n in range(6))
    r = _dot(xr, wrkv_ref[0])
    k = _dot(xk, wrkv_ref[1])
    v = _dot(xv, wrkv_ref[2])
    if has_vres:
        mix = _sigmoid(v0_ref[...] + _dot(_bf(_dot(xv, vA_ref[...])), vB_ref[...]))
        v = v + (vf_ref[0] - v) * mix
    r_o[0] = r
    k_o[0] = k
    v_o[0] = v
    g_o[0] = _dot(_bf(_sigmoid(_dot(xg, gA_ref[...]))), gB_ref[...])
    kk = k * kk_ref[...]
    kkn_o[0] = kk * lax.rsqrt(jnp.maximum(_head_sum(kk * kk, ones_bd), 1e-24))
    wl = w0_ref[...] + _dot(_bf(jnp.tanh(_dot(xw, wA_ref[...]))), wB_ref[...])
    lw = -DECAY_CAP * _sigmoid(wl)
    a_sig = _sigmoid(a0_ref[...] + _dot(_bf(_dot(xa, aA_ref[...])), aB_ref[...]))
    lw0_o[0], lw1_o[0] = lw[:, :n_lora], lw[:, n_lora:]
    as0, as1 = a_sig[:, :n_lora], a_sig[:, n_lora:]
    as0_o[0], as1_o[0] = as0, as1
    kd_sum = k * (2.0 + (as0 + as1 - 2.0) * ka_ref[...])
    bonus_o[0] = _head_sum(r * kd_sum * rk_ref[...], ones_bd) * v


def _rw_proj(x, mod, ng, mu, w_rkv, w0, wA, wB, a0, aA, aB, gA, gB, k_k, k_a, r_k, vres):
    b, t, d = x.shape
    tq = min(ROW_TILE, t)
    nt = t // tq
    has_vres = vres is not None
    row = pl.BlockSpec((1, tq, d), lambda bi, i: (bi, i, 0))

    def full(a):
        nd = a.ndim
        return pl.BlockSpec(a.shape, lambda bi, i: (0,) * nd)

    halo = tq // 8
    n8 = t // 8
    in_specs = [
        row,
        pl.BlockSpec((1, 8, d), lambda bi, i: (bi, jnp.maximum(i * halo - 1, 0), 0)),
        pl.BlockSpec((1, 8, d), lambda bi, i: (bi, jnp.minimum((i + 1) * halo, n8 - 1), 0)),
        pl.BlockSpec((1, N_MOD, d), lambda bi, i: (bi, 0, 0)),
    ]
    params = [ng, mu, w_rkv, w0, wA, wB, a0, aA, aB, gA, gB, k_k, k_a, r_k]
    args = [x, x, x, mod] + params
    in_specs += [full(a) for a in params]
    if has_vres:
        v_first, v0, vA, vB = vres
        args += [v_first, v0, vA, vB]
        in_specs += [row, full(v0), full(vA), full(vB)]
    n_out = 10
    return pl.pallas_call(
        functools.partial(_rw_proj_kernel, has_vres),
        grid=(b, nt),
        in_specs=in_specs,
        out_specs=[row] * n_out,
        out_shape=[jax.ShapeDtypeStruct((b, t, d), F32)] * n_out,
        compiler_params=_cparams(("arbitrary", "arbitrary")),
        name="rw_proj",
    )(*args)


def _wkv_kernel(reverse, finish, r_ref, k_ref, v_ref, kk_ref, lw_ref, as_ref, ka_ref, tri_ref, *rest):
    if finish:
        ya_ref, bonus_ref, g_ref, x_ref, mod_ref, ng_ref, lnx_ref, wo_ref, o_ref, s_ref = rest
    else:
        o_ref, s_ref = rest
    L = WKV_CHUNK
    rows, d = lw_ref.shape[1], lw_ref.shape[2]
    n_chunk = rows // L
    n_tile = d // LANES

    @pl.when(pl.program_id(1) == 0)
    def _():
        s_ref[...] = jnp.zeros_like(s_ref)

    lw = lw_ref[0]
    hi = _bf(lw)
    rem = lw - hi.astype(F32)
    mid = _bf(rem)
    lo = _bf(rem - mid.astype(F32))
    tri = tri_ref[...]
    cs = _dot(tri, hi) + _dot(tri, mid) + _dot(tri, lo)

    w_in = jnp.exp(cs)
    w_inv = jnp.exp(-cs)
    w_ex = w_in * jnp.exp(-lw)
    ends = [c * L if reverse else (c + 1) * L - 1 for c in range(n_chunk)]
    dec = [jnp.exp(cs[e:e + 1]) for e in ends]
    kk = kk_ref[0]
    a_sig = as_ref[0]
    v = v_ref[0]
    a_t = -(kk * w_ex)
    r_t = r_ref[0] * w_in
    b_t = kk * a_sig * w_inv
    k_t = k_ref[0] * (1.0 + (a_sig - 1.0) * ka_ref[...]) * w_inv

    lane = lax.broadcasted_iota(jnp.int32, (L, LANES), 1)
    row = lax.broadcasted_iota(jnp.int32, (L, LANES), 0)
    col = lane & (L - 1)
    head0 = lane < RW_HEAD
    if reverse:
        strict, incl = row < col, row <= col
    else:
        strict, incl = row > col, row >= col
    eye2 = jnp.where(row == col, 1.0, 0.0)
    r2 = lax.broadcasted_iota(jnp.int32, (LANES, LANES), 0) >> HEAD_SHIFT
    c2 = lax.broadcasted_iota(jnp.int32, (LANES, LANES), 1) >> HEAD_SHIFT
    same_head = r2 == c2

    def bdiag(z):
        return _bf(jnp.concatenate([jnp.where(head0, z, 0.0), jnp.where(head0, 0.0, z)], axis=0))

    chunks = list(range(n_chunk - 1, -1, -1) if reverse else range(n_chunk))
    pairs = [(c, p) for c in chunks for p in range(n_tile)]
    idx = range(len(pairs))

    def cut(z):
        return [z[c * L:(c + 1) * L, p * LANES:(p + 1) * LANES] for c, p in pairs]

    def dots(lhs, w):
        return [_dot(lhs[i], w[i]) for i in idx]

    at, rt, bt, kt, vv = (cut(z) for z in (a_t, r_t, b_t, k_t, v))
    dec_t = [dec[c][:, p * LANES:(p + 1) * LANES] for c, p in pairs]
    bh = [bt[i] * dec_t[i] for i in idx]
    kh = [kt[i] * dec_t[i] for i in idx]
    vbd = [bdiag(vv[i]) for i in idx]
    gram = dots([_bf(jnp.concatenate([at[i], rt[i]], axis=0)) for i in idx],
                [jnp.concatenate([bdiag(bt[i]).T, bdiag(kt[i]).T], axis=1) for i in idx])
    a_ab = [jnp.where(strict, gram[i][:L, :LANES], 0.0) for i in idx]
    a_ak = [jnp.where(strict, gram[i][:L, LANES:], 0.0) for i in idx]
    a_rb = [_bf(jnp.where(incl, gram[i][L:, :LANES], 0.0)) for i in idx]
    a_rk = [jnp.where(incl, gram[i][L:, LANES:], 0.0) for i in idx]
    av = dots([_bf(jnp.concatenate([a_ak[i], a_rk[i]], axis=0)) for i in idx], vbd)
    akv = [av[i][:L] for i in idx]
    y0 = [av[i][L:] for i in idx]
    x_inv = [eye2 + jnp.where((row >> 1) == (col >> 1), a_ab[i], 0.0) for i in idx]
    lev = 1
    while (1 << lev) < L:
        sel = ((row >> (lev + 1)) == (col >> (lev + 1))) & ((row >> lev) != (col >> lev))
        step = dots([_bf(x_inv[i]) for i in idx], [bdiag(jnp.where(sel, a_ab[i], 0.0)) for i in idx])
        grow = dots([_bf(step[i]) for i in idx], [bdiag(x_inv[i]) for i in idx])
        x_inv = [x_inv[i] + grow[i] for i in idx]
        lev += 1
    sol = dots([_bf(x_inv[i]) for i in idx],
               [jnp.concatenate([bdiag(at[i]), bdiag(akv[i])], axis=1) for i in idx])
    at2 = [sol[i][:, :LANES] for i in idx]
    u0 = [sol[i][:, LANES:] for i in idx]
    mix = dots(a_rb, [jnp.concatenate([bdiag(at2[i]), bdiag(u0[i])], axis=1) for i in idx])
    r2_ = [_bf(rt[i] + mix[i][:, :LANES]) for i in idx]
    y0 = [y0[i] + mix[i][:, LANES:] for i in idx]
    p_bd = [_bf(jnp.where(same_head, _dot_tn(_bf(at2[i]), _bf(bh[i])), 0.0)) for i in idx]
    q_bd = [jnp.where(same_head,
                      _dot_tn(_bf(jnp.concatenate([u0[i], vv[i]], axis=0)),
                              _bf(jnp.concatenate([bh[i], kh[i]], axis=0))), 0.0) for i in idx]
    state = [s_ref[p] for p in range(n_tile)]
    y = {}
    for i, (c, p) in enumerate(pairs):
        s0 = state[p]
        s0b = _bf(s0)
        y[c, p] = _dot(r2_[i], s0b.T) + y0[i]
        state[p] = s0 * dec_t[i] + _dot(s0b, p_bd[i]) + q_bd[i]
    for p in range(n_tile):
        s_ref[p] = state[p]
    y = jnp.concatenate([jnp.concatenate([y[c, p] for p in range(n_tile)], axis=1) for c in range(n_chunk)], axis=0)
    if finish:
        o_ref[0] = _rw_finish(y + ya_ref[0], bonus_ref[0], g_ref[0], x_ref[0], mod_ref[0][2:3], ng_ref[...],
                              lnx_ref[...], wo_ref[...])
    else:
        o_ref[0] = y


def _rw_finish(y, bonus, g, x, gate, ng, lnx, w_o):
    ones_bd = _head_ones()
    inv_n = 1.0 / RW_HEAD
    mean = _head_sum(y, ones_bd) * inv_n
    yc = y - mean
    var = _head_sum(yc * yc, ones_bd) * inv_n
    yn = yc * lax.rsqrt(var + GN_EPS) * lnx[0:1] + lnx[1:2]
    out = _dot(_bf((yn + bonus) * g), w_o)
    return x + gate * _rms(out, ng)


def _wkv_tri(rows, reverse):
    t = jnp.arange(rows)
    same = (t[:, None] // WKV_CHUNK) == (t[None, :] // WKV_CHUNK)
    tri = (t[:, None] <= t[None, :]) if reverse else (t[:, None] >= t[None, :])
    return (same & tri).astype(BF16)


def _wkv(r, k, v, kk, lw, a_sig, k_a, reverse, finish=None):
    b, t, d = r.shape
    rows = min(WKV_BLOCK, t)
    nb = t // rows
    if reverse:
        blk = pl.BlockSpec((1, rows, d), lambda bi, i: (bi, nb - 1 - i, 0))
    else:
        blk = pl.BlockSpec((1, rows, d), lambda bi, i: (bi, i, 0))

    def full(a):
        nd = a.ndim
        return pl.BlockSpec(a.shape, lambda bi, i: (0,) * nd)

    tri = _wkv_tri(rows, reverse)
    args = [r, k, v, kk, lw, a_sig, k_a, tri]
    in_specs = [blk] * 6 + [full(k_a), full(tri)]
    if finish is not None:
        y_other, bonus, g, x, mod, ng, lnx, w_o = finish
        args += [y_other, bonus, g, x, mod, ng, lnx, w_o]
        in_specs += [blk] * 4 + [pl.BlockSpec((1, N_MOD, d), lambda bi, i: (bi, 0, 0)), full(ng), full(lnx), full(w_o)]
    return pl.pallas_call(
        functools.partial(_wkv_kernel, reverse, finish is not None),
        grid=(b, nb),
        in_specs=in_specs,
        out_specs=blk,
        out_shape=jax.ShapeDtypeStruct((b, t, d), F32),
        scratch_shapes=[pltpu.VMEM((d // LANES, LANES, LANES), F32)],
        compiler_params=_cparams(("arbitrary", "arbitrary")),
        name="wkv_bwd" if reverse else "wkv_fwd",
    )(*args)


def _rope(x, cos, sin_signed):
    return x * cos + pltpu.roll(x, AT_HEAD // 2, axis=1) * sin_signed


def _at_qkv_kernel(x_ref, mod_ref, ng_ref, w_ref, qkg_ref, cos_ref, sin_ref, q_o, k_o, vt_o):
    mod = mod_ref[0]
    h = _norm_mod(x_ref[0], ng_ref[...], mod[0:1], mod[1:2])
    qkv = _dot(_bf(h), w_ref[...])
    nq = q_o.shape[2]
    nk = k_o.shape[2]
    cos, sin = cos_ref[...], sin_ref[...]
    qkg = qkg_ref[...]
    scale = AT_HEAD ** -0.5 * LOG2_E

    def head(z, gain, mul):
        ms = jnp.mean(z * z, axis=-1, keepdims=True)
        return _rope(z * lax.rsqrt(ms + NORM_EPS) * gain, cos, sin) * mul

    for n in range(nq // AT_HEAD):
        sl = slice(n * AT_HEAD, (n + 1) * AT_HEAD)
        q_o[0, :, sl] = _bf(head(qkv[:, sl], qkg[0:1], scale))
    for n in range(nk // AT_HEAD):
        sl = slice(n * AT_HEAD, (n + 1) * AT_HEAD)
        k_o[0, :, sl] = _bf(head(qkv[:, nq + n * AT_HEAD:nq + (n + 1) * AT_HEAD], qkg[1:2], 1.0))
    vt_o[0] = _bf(qkv[:, nq + nk:].T)


def _at_qkv(x, mod, ng, w_qkv, qk_g, cos, sin):
    b, t, d = x.shape
    tq = min(ROW_TILE, t)
    nk = AT_KVH * AT_HEAD
    row = lambda n: pl.BlockSpec((1, tq, n), lambda bi, i: (bi, i, 0))
    return pl.pallas_call(
        _at_qkv_kernel,
        grid=(b, t // tq),
        in_specs=[
            row(d),
            pl.BlockSpec((1, N_MOD, d), lambda bi, i: (bi, 0, 0)),
            pl.BlockSpec((1, d), lambda bi, i: (0, 0)),
            pl.BlockSpec(w_qkv.shape, lambda bi, i: (0, 0)),
            pl.BlockSpec((2, AT_HEAD), lambda bi, i: (0, 0)),
            pl.BlockSpec((tq, AT_HEAD), lambda bi, i: (i, 0)),
            pl.BlockSpec((tq, AT_HEAD), lambda bi, i: (i, 0)),
        ],
        out_specs=[row(d), row(nk), pl.BlockSpec((1, nk, tq), lambda bi, i: (bi, 0, i))],
        out_shape=[jax.ShapeDtypeStruct((b, t, d), BF16), jax.ShapeDtypeStruct((b, t, nk), BF16),
                   jax.ShapeDtypeStruct((b, nk, t), BF16)],
        compiler_params=_cparams(("arbitrary", "arbitrary")),
        name="at_qkv",
    )(x, mod, ng, w_qkv, qk_g, cos, sin)


def _attn_kernel(q_ref, k_ref, vt_ref, x_ref, mod_ref, ng_ref, wo_ref, o_ref):
    n_head = q_ref.shape[2] // AT_HEAD
    group = n_head // AT_KVH
    t = k_ref.shape[1]
    kb = min(ATT_K_BLOCK, t)
    n_kb = t // kb
    ones = jnp.ones((BF16_ROWS, t), BF16)
    vt = [jnp.concatenate([vt_ref[0, h * AT_HEAD:(h + 1) * AT_HEAD], ones], axis=0) for h in range(AT_KVH)]
    blocks = [(n, b) for n in range(n_head) for b in range(n_kb)]

    def scores(n, b):
        h = n // group
        return _dot_nt(k_ref[0, b * kb:(b + 1) * kb, h * AT_HEAD:(h + 1) * AT_HEAD],
                       q_ref[0, :, n * AT_HEAD:(n + 1) * AT_HEAD])

    pending = [scores(*blk) for blk in blocks[:ATT_AHEAD]]
    m = acc = None
    heads = []
    for i, (n, b) in enumerate(blocks):
        if i + ATT_AHEAD < len(blocks):
            pending.append(scores(*blocks[i + ATT_AHEAD]))
        st = pending.pop(0)
        m_blk = jnp.max(st, axis=0, keepdims=True)
        if b == 0:
            m = m_blk
        else:
            m_new = jnp.maximum(m, m_blk)
            acc = acc * jnp.exp2(m - m_new)
            m = m_new
        part = _dot(vt[n // group][:, b * kb:(b + 1) * kb], jnp.exp2(_bf(st - m)))
        acc = part if b == 0 else acc + part
        if b == n_kb - 1:
            heads.append(_bf((acc[:AT_HEAD] / acc[AT_HEAD:AT_HEAD + 1]).T))
    out = _dot(jnp.concatenate(heads, axis=1), wo_ref[...])
    o_ref[0] = x_ref[0] + mod_ref[0][2:3] * _rms(out, ng_ref[...])


def _attention(q, k, vt, x, mod, ng, w_o):
    b, t, d = q.shape
    tq = min(ATT_Q_TILE, t)
    nk = k.shape[2]
    row = pl.BlockSpec((1, tq, d), lambda bi, i: (bi, i, 0))
    return pl.pallas_call(
        _attn_kernel,
        grid=(b, t // tq),
        in_specs=[row,
                  pl.BlockSpec((1, t, nk), lambda bi, i: (bi, 0, 0)),
                  pl.BlockSpec((1, nk, t), lambda bi, i: (bi, 0, 0)),
                  row,
                  pl.BlockSpec((1, N_MOD, d), lambda bi, i: (bi, 0, 0)),
                  pl.BlockSpec((1, d), lambda bi, i: (0, 0)),
                  pl.BlockSpec((d, d), lambda bi, i: (0, 0))],
        out_specs=row,
        out_shape=jax.ShapeDtypeStruct((b, t, d), F32),
        compiler_params=_cparams(("arbitrary", "arbitrary")),
        name="attention",
    )(q, k, vt, x, mod, ng, w_o)


def _ffn_kernel(x_ref, xp_ref, xn_ref, mod_ref, ng_ref, wu_ref, cw_ref, cb_ref, wd_ref, o_ref):
    i = pl.program_id(1)
    nt = pl.num_programs(1)
    rows = x_ref.shape[1]
    f_dim = wd_ref.shape[0]
    ft = min(FFN_TILE, f_dim)
    mod = mod_ref[0]
    ng = ng_ref[...]
    x = x_ref[0]
    h_all = _bf(_norm_mod(jnp.concatenate([xp_ref[0], x, xn_ref[0]], axis=0), ng[0:1], mod[3:4], mod[4:5]))
    n_f = f_dim // ft
    sub = min(FFN_SUB_ROWS, rows)
    n_sub = rows // sub
    top, bot = BF16_ROWS, BF16_ROWS + sub

    for s in range(n_sub):
        h = h_all[s * sub:s * sub + sub + 2 * BF16_ROWS]
        pad_top = (i == 0) if s == 0 else False
        pad_bot = (i == nt - 1) if s == n_sub - 1 else False

        def up(f, h=h):
            cols = slice(f * ft, (f + 1) * ft)
            return _dot(h, wu_ref[0, :, cols]), _dot(h[top:bot], wu_ref[1, :, cols])

        pending = [up(f) for f in range(min(FFN_AHEAD, n_f))]
        acc = None
        for f in range(n_f):
            if f + FFN_AHEAD < n_f:
                pending.append(up(f + FFN_AHEAD))
            u, gv = pending.pop(0)
            cols = slice(f * ft, (f + 1) * ft)
            u = jnp.concatenate([jnp.where(pad_top, 0.0, u[:top]), u[top:bot], jnp.where(pad_bot, 0.0, u[bot:])], axis=0)
            cw = cw_ref[:, cols]
            uc = (cw[0:1] * pltpu.roll(u, 1, axis=0)[top:bot] + cw[1:2] * u[top:bot]
                  + cw[2:3] * pltpu.roll(u, u.shape[0] - 1, axis=0)[top:bot] + cb_ref[:, cols])
            act = uc * _sigmoid(uc) * gv
            part = _dot(_bf(act), wd_ref[cols, :])
            acc = part if acc is None else acc + part
        rs = slice(s * sub, (s + 1) * sub)
        o_ref[0, rs] = x[rs] + mod[5:6] * _rms(acc, ng[1:2])


def _ffn(x, mod, ng, w_up, conv_w, conv_b, w_down):
    b, t, d = x.shape
    f_dim = w_down.shape[0]
    rows = min(FFN_ROWS, t)
    halo = rows // BF16_ROWS
    n_halo = t // BF16_ROWS

    def full(a):
        nd = a.ndim
        return pl.BlockSpec(a.shape, lambda bi, i: (0,) * nd, pipeline_mode=pl.Buffered(1))

    conv_b = conv_b.reshape(1, f_dim)
    row = pl.BlockSpec((1, rows, d), lambda bi, i: (bi, i, 0))
    return pl.pallas_call(
        _ffn_kernel,
        grid=(b, t // rows),
        in_specs=[
            row,
            pl.BlockSpec((1, BF16_ROWS, d), lambda bi, i: (bi, jnp.maximum(i * halo - 1, 0), 0)),
            pl.BlockSpec((1, BF16_ROWS, d), lambda bi, i: (bi, jnp.minimum((i + 1) * halo, n_halo - 1), 0)),
            pl.BlockSpec((1, N_MOD, d), lambda bi, i: (bi, 0, 0)),
            full(ng), full(w_up), full(conv_w), full(conv_b), full(w_down),
        ],
        out_specs=row,
        out_shape=jax.ShapeDtypeStruct((b, t, d), F32),
        compiler_params=_cparams(("arbitrary", "arbitrary")),
        name="ffn",
    )(x, x, x, mod, ng, w_up, conv_w, conv_b, w_down)


def _rope_tables(t):
    pos = jnp.arange(t)
    row = (pos // GRID_W).astype(F32)
    col = (pos % GRID_W).astype(F32)
    n_pair = AT_HEAD // 4
    inv = ROPE_THETA ** (-jnp.arange(n_pair, dtype=F32) / n_pair)
    ang = jnp.concatenate([row[:, None] * inv, col[:, None] * inv], axis=-1)
    cos, sin = jnp.cos(ang), jnp.sin(ang)
    return jnp.concatenate([cos, cos], axis=-1), jnp.concatenate([-sin, sin], axis=-1)


def _deinterleave_heads(w_qkv, qk_g, d):
    half = jnp.arange(AT_HEAD // 2)
    perm = jnp.concatenate([2 * half, 2 * half + 1])
    n_rot = d + AT_KVH * AT_HEAD
    cols = (jnp.arange(n_rot // AT_HEAD)[:, None] * AT_HEAD + perm[None, :]).reshape(-1)
    cols = jnp.concatenate([cols, jnp.arange(n_rot, w_qkv.shape[1])])
    return w_qkv[:, cols], qk_g[:, perm]


def _trunk(x, mod_all, cos, sin, norm_g, rw, at, ffn):
    v_first = None
    for l in range(mod_all.shape[0]):
        mod = mod_all[l]
        ng = norm_g[l]
        j = l // 2
        if l % 2 == 0:
            proj, vres, ka, lnx, w_o = rw[j]
            if vres is not None:
                vres = (v_first,) + vres
            r, k, v, kk, g, bonus, lw0, lw1, as0, as1 = _rw_proj(x, mod, ng[0:1], *proj, vres)
            if j == 0:
                v_first = v
            y = _wkv(r, k, v, kk, lw0, as0, ka, reverse=False)
            x = _wkv(r, k, v, kk, lw1, as1, ka, reverse=True, finish=(y, bonus, g, x, mod, ng[1:2], lnx, w_o))
        else:
            w_qkv, qk_g, w_o = at[j]
            q, k, vt = _at_qkv(x, mod, ng[0:1], w_qkv, qk_g, cos, sin)
            x = _attention(q, k, vt, x, mod, ng[1:2], w_o)
        x = _ffn(x, mod, ng[2:4], *ffn[l])
    return x


def kernel(x_prompt, x_sample, c_prompt, c_sample, ada_w, ada_b, norm_g, rw_mu, rw_rkv, rw_w0, rw_wA, rw_wB, rw_a0,
           rw_aA, rw_aB, rw_gA, rw_gB, rw_kk, rw_ka, rw_rk, rw_lnx, rw_o, rw_v0, rw_vA, rw_vB, at_qkv, at_qk_g,
           at_o, ffn_up, ffn_conv_w, ffn_conv_b, ffn_down):
    n_prompt = x_prompt.shape[0]
    d = x_prompt.shape[2]
    depth = ada_w.shape[0]
    f_dim = ffn_down.shape[1]
    row = lambda z: z.reshape(1, d)

    def both_dirs(bias, w_in, w_out):
        zero = jnp.zeros_like(w_out[0])
        w_bd = jnp.concatenate([jnp.concatenate([w_out[0], zero], axis=1),
                                jnp.concatenate([zero, w_out[1]], axis=1)], axis=0)
        return bias.reshape(1, 2 * d), _bf(jnp.concatenate([w_in[0], w_in[1]], axis=1)), _bf(w_bd)

    rw = []
    for j in range(rw_rkv.shape[0]):
        proj = (rw_mu[j], _bf(rw_rkv[j]), *both_dirs(rw_w0[j], rw_wA[j], rw_wB[j]),
                *both_dirs(rw_a0[j], rw_aA[j], rw_aB[j]), _bf(rw_gA[j]), _bf(rw_gB[j]), row(rw_kk[j]),
                row(rw_ka[j]), row(rw_rk[j]))
        vres = None if j == 0 else (row(rw_v0[j - 1]), _bf(rw_vA[j - 1]), _bf(rw_vB[j - 1]))
        rw.append((proj, vres, row(rw_ka[j]), rw_lnx[j], _bf(rw_o[j])))
    at = []
    for j in range(at_qkv.shape[0]):
        w_qkv, qk_g = _deinterleave_heads(at_qkv[j], at_qk_g[j], d)
        at.append((_bf(w_qkv), qk_g, _bf(at_o[j])))
    ffn = [(_bf(ffn_up[l]).reshape(d, 2, f_dim).transpose(1, 0, 2), ffn_conv_w[l], ffn_conv_b[l], _bf(ffn_down[l]))
           for l in range(depth)]

    c = jnp.concatenate([c_prompt, c_sample], axis=0)
    mod_all = _ada_mod(c, ada_w, ada_b).reshape(depth, c.shape[0], N_MOD, d)
    outs = []
    for x, mod in ((x_prompt, mod_all[:, :n_prompt]), (x_sample, mod_all[:, n_prompt:])):
        cos, sin = _rope_tables(x.shape[1])
        outs.append(_trunk(x, mod, cos, sin, norm_g, rw, at, ffn))
    return tuple(outs)
```
